```python
import numpy as np
import jax
import jax.numpy as jnp
from jax import lax

D_MODEL = 1024
BATCH = 32
SEQ = 2048
DEPTH = 4
DEC_BATCH = 32
DEC_SEQ = 64
PAST_LEN = 4096

CHUNK = 64
N_HEADS = 8
HEAD_DIM = 64
D_ATTN = N_HEADS * HEAD_DIM
IDX_HEADS = 8
IDX_DIM = 64
TOPK_MAX = 256
D_CONF = 512
CONF_WIDTH = 31
D_SC = 512
SC_WIDTH = 3
N_BRANCH = 3
D_BRANCH = 512
D_FF = 2816
N_EXPERTS = 8
TOP_K_EXPERTS = 2
D_FF_EXPERT = 1408
N_DENSE = (DEPTH + 1) // 2
N_MOE = DEPTH // 2
EPS = 1e-6
SPLIT_SIZES = (D_ATTN, HEAD_DIM, HEAD_DIM, IDX_HEADS * IDX_DIM, IDX_DIM, IDX_HEADS,
               2 * D_CONF, 3 * D_SC, N_BRANCH * D_MODEL)
D_IN = D_ATTN + 2 * HEAD_DIM + IDX_HEADS * IDX_DIM + IDX_DIM + IDX_HEADS + 2 * D_CONF + 3 * D_SC + N_BRANCH * D_MODEL

kernel_name = "streaming_hybrid_dsa_conformer_shortconv"


def _rmsnorm(x, g):
    xf = x.astype(jnp.float32)
    xf = xf * lax.rsqrt(jnp.mean(xf * xf, axis=-1, keepdims=True) + EPS)
    return xf.astype(x.dtype) * g


def _layernorm(x, g, b):
    xf = x.astype(jnp.float32)
    mu = jnp.mean(xf, axis=-1, keepdims=True)
    xc = xf - mu
    var = jnp.mean(xc * xc, axis=-1, keepdims=True)
    return (xc * lax.rsqrt(var + EPS)).astype(x.dtype) * g + b


def _dwconv_valid(xp, w):
    return lax.conv_general_dilated(
        xp, w.astype(xp.dtype)[:, None, :], window_strides=(1,), padding="VALID",
        dimension_numbers=("NWC", "WIO", "NWC"), feature_group_count=xp.shape[-1])


def _sparse_attend(q, iq, iw, k_all, v_all, ik_all, limit, topk):
    key_pos = jnp.arange(k_all.shape[1])
    rel = jax.nn.relu(jnp.einsum("bqhe,ble->bqhl", iq, ik_all).astype(jnp.float32) * IDX_DIM ** -0.5)
    score = jnp.einsum("bqh,bqhl->bql", iw.astype(jnp.float32) * IDX_HEADS ** -0.5, rel)
    score = jnp.where(key_pos < limit, score, -jnp.inf)
    _, sel = lax.top_k(score, topk)
    kv = jnp.concatenate([k_all, v_all], axis=-1)
    kv_sel = jax.vmap(lambda rows, ids: rows[ids])(kv, sel)
    k_sel, v_sel = jnp.split(kv_sel, 2, axis=-1)
    logits = jnp.einsum("bqhd,bqkd->bqhk", q, k_sel).astype(jnp.float32) * HEAD_DIM ** -0.5
    logits = jnp.where((sel < limit)[:, :, None, :], logits, -jnp.inf)
    p = jax.nn.softmax(logits, axis=-1).astype(v_sel.dtype)
    return jnp.einsum("bqhk,bqkd->bqhd", p, v_sel)


def _token_mixer(h, w_in, w_dw_conf, b_dw_conf, ln_conf_g, ln_conf_b, w_dw_short, w_branch, w_out, past):
    B, T, _ = h.shape
    offsets = np.cumsum(SPLIT_SIZES)[:-1].tolist()
    q, k, v, iq, ik, iw, glu, sc, gates = jnp.split(h @ w_in, offsets, axis=-1)
    q = q.reshape(B, T, N_HEADS, HEAD_DIM)
    iq = iq.reshape(B, T, IDX_HEADS, IDX_DIM)
    if past is None:
        n_blk = T // CHUNK
        topk = min(TOPK_MAX, T // 4)

        def blocks(a):
            return jnp.moveaxis(a.reshape((B, n_blk, CHUNK) + a.shape[2:]), 1, 0)

        def attend_block(args):
            qb, iqb, iwb, j = args
            return _sparse_attend(qb, iqb, iwb, k, v, ik, (j + 1) * CHUNK, topk)

        attn = lax.map(attend_block, (blocks(q), blocks(iq), blocks(iw), jnp.arange(n_blk)))
        attn = jnp.moveaxis(attn, 0, 1).reshape(B, T, D_ATTN)
        conf_hist = jnp.zeros((B, CONF_WIDTH - 1, D_CONF), h.dtype)
        short_hist = jnp.zeros((B, SC_WIDTH - 1, D_SC), h.dtype)
    else:
        cache_k, cache_v, cache_ik, conf_hist, short_hist = past
        k_all = jnp.concatenate([cache_k, k], axis=1)
        v_all = jnp.concatenate([cache_v, v], axis=1)
        ik_all = jnp.concatenate([cache_ik, ik], axis=1)
        n_keys = k_all.shape[1]
        attn = _sparse_attend(q, iq, iw, k_all, v_all, ik_all, n_keys,
                              min(TOPK_MAX, n_keys // 4)).reshape(B, T, D_ATTN)
    a, g = jnp.split(glu, 2, axis=-1)
    u_conf = jnp.concatenate([conf_hist, a * jax.nn.sigmoid(g)], axis=1)
    conv = _dwconv_valid(u_conf, w_dw_conf) + b_dw_conf
    conf = jax.nn.silu(_layernorm(conv, ln_conf_g, ln_conf_b))
    b_gate, c_gate, xs = jnp.split(sc, 3, axis=-1)
    u_short = jnp.concatenate([short_hist, c_gate * xs], axis=1)
    short = b_gate * _dwconv_valid(u_short, w_dw_short)
    g_a, g_b, g_c = jnp.split(jax.nn.sigmoid(gates), 3, axis=-1)
    merged = g_a * (attn @ w_branch[0]) + g_b * (conf @ w_branch[1]) + g_c * (short @ w_branch[2])
    out = merged @ w_out
    new_state = (k, v, ik, u_conf[:, -(CONF_WIDTH - 1):], u_short[:, -(SC_WIDTH - 1):])
    return out, new_state


def _swiglu(h, w_in, w_out):
    gate, up = jnp.split(h @ w_in, 2, axis=-1)
    return (jax.nn.silu(gate) * up) @ w_out


def _moe_swiglu(h, w_router, b_router, w_exp_in, w_exp_out):
    logits = (h @ w_router).astype(jnp.float32) + b_router
    top_val, top_idx = lax.top_k(logits, TOP_K_EXPERTS)
    top_w = jax.nn.softmax(top_val, axis=-1)
    combine = jnp.sum(jax.nn.one_hot(top_idx, N_EXPERTS, dtype=jnp.float32) * top_w[..., None], axis=-2)
    combine = combine.astype(h.dtype)
    out = jnp.zeros_like(h)
    for e in range(N_EXPERTS):
        out = out + combine[..., e:e + 1] * _swiglu(h, w_exp_in[e], w_exp_out[e])
    return out


def _trunk(x, c, past, p):
    c_act = jax.nn.silu(c)
    states = [[], [], [], [], []]
    for l in range(DEPTH):
        mod = (c_act @ p["w_ada"][l] + p["b_ada"][l])[:, None, :]
        shift_m, scale_m, gate_m, shift_f, scale_f, gate_f = jnp.split(mod, 6, axis=-1)
        h = _rmsnorm(x, p["g_norm_mix"][l]) * (1 + scale_m) + shift_m
        layer_past = None if past is None else (past[0][l], past[1][l], past[2][l], past[3][l], past[4][l])
        m, st = _token_mixer(h, p["w_in"][l], p["w_dw_conf"][l], p["b_dw_conf"][l], p["ln_conf_g"][l],
                             p["ln_conf_b"][l], p["w_dw_short"][l], p["w_branch"][l], p["w_out"][l], layer_past)
        for i in range(5):
            states[i].append(st[i])
        x = x + gate_m * m
        h = _rmsnorm(x, p["g_norm_ffn"][l]) * (1 + scale_f) + shift_f
        if l % 2 == 0:
            f = _swiglu(h, p["w_ffn_in"][l // 2], p["w_ffn_out"][l // 2])
        else:
            f = _moe_swiglu(h, p["w_router"][l // 2], p["b_router"][l // 2],
                            p["w_exp_in"][l // 2], p["w_exp_out"][l // 2])
        x = x + gate_f * f
    y = _rmsnorm(x, p["g_final"])
    return y, (jnp.stack(states[0]), jnp.stack(states[1]), jnp.stack(states[2]),
               jnp.stack(states[3]), jnp.stack(states[4]))


def setup_inputs(seed: int = 0) -> dict:
    key = jax.random.key(seed)
    ks = jax.random.split(key, 28)

    def nrm(k, shape, scale):
        return jax.random.normal(k, shape, jnp.float32) * scale

    return {
        "x_prompt": nrm(ks[0], (BATCH, SEQ, D_MODEL), 1.0),
        "x_sample": nrm(ks[1], (DEC_BATCH, DEC_SEQ, D_MODEL), 1.0),
        "cache_k": nrm(ks[2], (DEPTH, DEC_BATCH, PAST_LEN, HEAD_DIM), 1.0),
        "cache_v": nrm(ks[3], (DEPTH, DEC_BATCH, PAST_LEN, HEAD_DIM), 1.0),
        "cache_idx_k": nrm(ks[4], (DEPTH, DEC_BATCH, PAST_LEN, IDX_DIM), 1.0),
        "state_conv_conformer": nrm(ks[5], (DEPTH, DEC_BATCH, CONF_WIDTH - 1, D_CONF), 0.5),
        "state_conv_short": nrm(ks[6], (DEPTH, DEC_BATCH, SC_WIDTH - 1, D_SC), 1.0),
        "c_prompt": nrm(ks[7], (BATCH, D_MODEL), 1.0),
        "c_sample": nrm(ks[8], (DEC_BATCH, D_MODEL), 1.0),
        "w_ada": nrm(ks[9], (DEPTH, D_MODEL, 6 * D_MODEL), 0.3 * D_MODEL ** -0.5),
        "b_ada": nrm(ks[10], (DEPTH, 6 * D_MODEL), 0.02),
        "g_norm_mix": 1.0 + nrm(ks[11], (DEPTH, D_MODEL), 0.02),
        "g_norm_ffn": 1.0 + nrm(ks[12], (DEPTH, D_MODEL), 0.02),
        "w_in": nrm(ks[13], (DEPTH, D_MODEL, D_IN), D_MODEL ** -0.5),
        "w_dw_conf": nrm(ks[14], (DEPTH, CONF_WIDTH, D_CONF), CONF_WIDTH ** -0.5),
        "b_dw_conf": nrm(ks[15], (DEPTH, D_CONF), 0.02),
        "ln_conf_g": 1.0 + nrm(ks[16], (DEPTH, D_CONF), 0.02),
        "ln_conf_b": nrm(ks[17], (DEPTH, D_CONF), 0.02),
        "w_dw_short": nrm(ks[18], (DEPTH, SC_WIDTH, D_SC), SC_WIDTH ** -0.5),
        "w_branch": nrm(ks[19], (DEPTH, N_BRANCH, D_BRANCH, D_MODEL), D_BRANCH ** -0.5),
        "w_out": nrm(ks[20], (DEPTH, D_MODEL, D_MODEL), D_MODEL ** -0.5),
        "w_ffn_in": nrm(ks[21], (N_DENSE, D_MODEL, 2 * D_FF), D_MODEL ** -0.5),
        "w_ffn_out": nrm(ks[22], (N_DENSE, D_FF, D_MODEL), D_FF ** -0.5),
        "w_router": nrm(ks[23], (N_MOE, D_MODEL, N_EXPERTS), D_MODEL ** -0.5),
        "b_router": nrm(ks[24], (N_MOE, N_EXPERTS), 0.01),
        "w_exp_in": nrm(ks[25], (N_MOE, N_EXPERTS, D_MODEL, 2 * D_FF_EXPERT), D_MODEL ** -0.5),
        "w_exp_out": nrm(ks[26], (N_MOE, N_EXPERTS, D_FF_EXPERT, D_MODEL), D_FF_EXPERT ** -0.5),
        "g_final": 1.0 + nrm(ks[27], (D_MODEL,), 0.02),
    }


def reference(x_prompt, x_sample, cache_k, cache_v, cache_idx_k, state_conv_conformer, state_conv_short,
              c_prompt, c_sample, w_ada, b_ada, g_norm_mix, g_norm_ffn, w_in, w_dw_conf, b_dw_conf,
              ln_conf_g, ln_conf_b, w_dw_short, w_branch, w_out, w_ffn_in, w_ffn_out, w_router, b_router,
              w_exp_in, w_exp_out, g_final):
    params = {
        "w_ada": w_ada, "b_ada": b_ada, "g_norm_mix": g_norm_mix, "g_norm_ffn": g_norm_ffn,
        "w_in": w_in, "w_dw_conf": w_dw_conf, "b_dw_conf": b_dw_conf, "ln_conf_g": ln_conf_g,
        "ln_conf_b": ln_conf_b, "w_dw_short": w_dw_short, "w_branch": w_branch, "w_out": w_out,
        "w_ffn_in": w_ffn_in, "w_ffn_out": w_ffn_out, "w_router": w_router, "b_router": b_router,
        "w_exp_in": w_exp_in, "w_exp_out": w_exp_out, "g_final": g_final,
    }
    y_prompt, (pk, pv, pik, pconf, pshort) = _trunk(x_prompt, c_prompt, None, params)
    y_sample, (sk, sv, sik, sconf, sshort) = _trunk(
        x_sample, c_sample, (cache_k, cache_v, cache_idx_k, state_conv_conformer, state_conv_short), params)
    return (y_prompt, y_sample, pk, pv, pik, pconf, pshort, sk, sv, sik, sconf, sshort)
```

```python
import functools

import jax
import jax.numpy as jnp
import numpy as np
from jax import lax
from jax.experimental import pallas as pl
from jax.experimental.pallas import tpu as pltpu

CHUNK = 64
N_HEADS = 8
HEAD_DIM = 64
IDX_HEADS = 8
IDX_DIM = 64
TOPK_MAX = 256
D_CONF = 512
CONF_WIDTH = 31
D_SC = 512
SC_WIDTH = 3
N_BRANCH = 3
N_EXPERTS = 8
EPS = 1e-6

LANES = 128
SUBLANES = 8
VMEM_LIMIT = 56 * 1024 * 1024
INT_MIN = np.int32(-2 ** 31)

F32 = jnp.float32
BF16 = jnp.bfloat16

SEC_Q = 0
SEC_IQ = 512
SEC_KK = 1024
SEC_VV = 1152
SEC_IK = 1280
SEC_IW = 1408
SEC_GLU_A = 1536
SEC_GLU_G = 2048
SEC_SC_B = 2560
SEC_SC_C = 3072
SEC_SC_X = 3584
SEC_GATES = 4096
P_WIDTH = 7168
HIST_CONF = 32
HIST_SHORT = 8


def _cparams(*sem):
    return pltpu.CompilerParams(dimension_semantics=sem, vmem_limit_bytes=VMEM_LIMIT)


def _sigmoid(x):
    return 1.0 / (1.0 + jnp.exp(-x))


def _silu(x):
    return x * _sigmoid(x)


def _token_blocks(B, T, rows=1024):
    if T >= rows:
        assert T % rows == 0
        return 1, rows
    nb = min(B, rows // T)
    assert B % nb == 0
    return nb, T


def _ada_kernel(c_ref, w_ref, b_ref, o_ref):
    c = _silu(c_ref[...]).astype(BF16)
    o_ref[...] = jnp.dot(c, w_ref[...].astype(BF16), preferred_element_type=F32) + b_ref[...]


def _ada(c, w_ada, b_ada):
    depth, d, n = w_ada.shape
    bc = c.shape[0]
    tn = 1536
    assert n % tn == 0
    return pl.pallas_call(
        _ada_kernel,
        grid=(depth, n // tn),
        in_specs=[
            pl.BlockSpec((bc, d), lambda l, j: (0, 0)),
            pl.BlockSpec((None, d, tn), lambda l, j: (l, 0, j)),
            pl.BlockSpec((None, 1, tn), lambda l, j: (l, 0, j)),
        ],
        out_specs=pl.BlockSpec((None, bc, tn), lambda l, j: (l, 0, j)),
        out_shape=jax.ShapeDtypeStruct((depth, bc, n), F32),
        compiler_params=_cparams("arbitrary", "arbitrary"),
        name="ada",
    )(c, w_ada, b_ada.reshape(depth, 1, n))


def _norm_mod(x, g, scale, shift):
    ms = jnp.mean(x * x, axis=-1, keepdims=True)
    xn = (x * lax.rsqrt(ms + EPS)) * g
    h = xn * (1.0 + scale) + shift
    nb, tt, d = x.shape
    return h.reshape(nb * tt, d).astype(BF16)


def _mod_spec(nb, d, which, bmap):
    return pl.BlockSpec((nb, None, 1, d), lambda *a: (bmap(*a), which, 0, 0))


def _inproj_kernel(x_ref, g_ref, sc_ref, sh_ref, w_ref, o_ref, h_scr):
    @pl.when(pl.program_id(2) == 0)
    def _():
        h_scr[...] = _norm_mod(x_ref[...], g_ref[...], sc_ref[...], sh_ref[...])

    nb, tt, tn = o_ref.shape
    o_ref[...] = jnp.dot(h_scr[...], w_ref[...], preferred_element_type=F32).reshape(nb, tt, tn)


def _inproj(x, g, mod, w):
    B, T, D = x.shape
    nb, tt = _token_blocks(B, T)
    tn = 1024
    width = w.shape[1]
    assert width % tn == 0
    grid = (B // nb, T // tt, width // tn)
    return pl.pallas_call(
        _inproj_kernel,
        grid=grid,
        in_specs=[
            pl.BlockSpec((nb, tt, D), lambda b, t, j: (b, t, 0)),
            pl.BlockSpec((1, D), lambda b, t, j: (0, 0)),
            _mod_spec(nb, D, 1, lambda b, t, j: b),
            _mod_spec(nb, D, 0, lambda b, t, j: b),
            pl.BlockSpec((D, tn), lambda b, t, j: (0, j)),
        ],
        out_specs=pl.BlockSpec((nb, tt, tn), lambda b, t, j: (b, t, j)),
        out_shape=jax.ShapeDtypeStruct((B, T, width), F32),
        scratch_shapes=[pltpu.VMEM((nb * tt, D), BF16)],
        compiler_params=_cparams("arbitrary", "arbitrary", "arbitrary"),
        name="inproj",
    )(x, g, mod, mod, w)


def _rearrange_w_in(w_in):
    d = w_in.shape[0]
    sizes = (N_HEADS * HEAD_DIM, HEAD_DIM, HEAD_DIM, IDX_HEADS * IDX_DIM, IDX_DIM, IDX_HEADS,
             2 * D_CONF, 3 * D_SC, N_BRANCH * d)
    offs = np.cumsum((0,) + sizes)
    q, k, v, iq, ik, iw, glu, sc, gates = [w_in[:, offs[i]:offs[i + 1]] for i in range(9)]
    iw_pad = jnp.zeros((d, LANES - IDX_HEADS), w_in.dtype)
    out = jnp.concatenate([q, iq, k, k, v, v, ik, ik, iw, iw_pad, glu, sc, gates], axis=1)
    assert out.shape[1] == P_WIDTH
    return out.astype(BF16)


def _stack_heads(x):
    rows = x.shape[0]
    lane = lax.broadcasted_iota(jnp.int32, (rows, LANES), 1)
    lo = lane < HEAD_DIM
    parts = []
    for h in range(N_HEADS):
        slab = x[:, (h // 2) * LANES:(h // 2 + 1) * LANES]
        parts.append(jnp.where(lo if h % 2 == 0 else jnp.logical_not(lo), slab, 0.0))
    return jnp.concatenate(parts, axis=0).astype(BF16)


def _count(mask):
    return jnp.sum(jnp.where(mask, 1.0, 0.0), axis=1, keepdims=True)


def _attn_kernel(q_ref, iq_ref, iw_ref, kk_ref, vv_ref, ik_ref, o_ref, key_scr, sel_scr, p_scr,
                 *, topk, n_keys, causal):
    R = CHUNK
    L = kk_ref.shape[0]
    nt = (((1,), (1,)), ((), ()))
    if causal:
        limit = (pl.program_id(1) + 1) * CHUNK
    else:
        limit = n_keys
    pos = lax.broadcasted_iota(jnp.int32, (R, L), 1)
    admissible = pos < limit

    rel = lax.dot_general(_stack_heads(iq_ref[...]), ik_ref[...].astype(BF16), nt,
                          preferred_element_type=F32)
    iw = iw_ref[...]
    score = jnp.zeros((R, L), F32)
    for h in range(IDX_HEADS):
        r = jnp.maximum(rel[h * R:(h + 1) * R] * (IDX_DIM ** -0.5), 0.0)
        score = score + (iw[:, h:h + 1] * (IDX_HEADS ** -0.5)) * r

    bits = lax.bitcast_convert_type(score, jnp.int32)
    key = jnp.where(bits < 0, bits ^ jnp.int32(0x7FFFFFFF), bits)
    key_scr[...] = jnp.where(admissible, key, INT_MIN)

    def search(i, thr):
        cand = thr + lax.shift_left(jnp.int32(1), 31 - i)
        cnt = _count(key_scr[...] >= cand)
        return jnp.where(cnt >= topk, cand, thr)

    thr = lax.fori_loop(0, 32, search, jnp.full((R, 1), INT_MIN, jnp.int32))
    keys = key_scr[...]
    n_ge = _count(keys >= thr)
    has_tie = jnp.max(n_ge) > topk

    @pl.when(jnp.logical_not(has_tie))
    def _():
        sel_scr[...] = jnp.where(jnp.logical_and(key_scr[...] >= thr, admissible), 1.0, 0.0)

    @pl.when(has_tie)
    def _():
        kk_ = key_scr[...]
        eq = kk_ == thr
        need = topk - _count(kk_ > thr)
        nbits = int(L - 1).bit_length()

        def tie_search(i, d):
            cand = d + lax.shift_left(jnp.int32(1), nbits - 1 - i)
            f = _count(jnp.logical_and(eq, pos < cand))
            return jnp.where(f < need, cand, d)

        d = lax.fori_loop(0, nbits, tie_search, jnp.zeros((R, 1), jnp.int32))
        chosen = jnp.logical_or(kk_ > thr, jnp.logical_and(eq, pos <= d))
        sel_scr[...] = jnp.where(jnp.logical_and(chosen, admissible), 1.0, 0.0)

    logits = lax.dot_general(_stack_heads(q_ref[...]), kk_ref[...].astype(BF16), nt,
                             preferred_element_type=F32) * (HEAD_DIM ** -0.5)
    sel = sel_scr[...] > 0.5
    denoms = []
    for h in range(N_HEADS):
        lg = jnp.where(sel, logits[h * R:(h + 1) * R], -jnp.inf)
        m = jnp.max(lg, axis=1, keepdims=True)
        p = jnp.exp(lg - m)
        denoms.append(jnp.sum(p, axis=1, keepdims=True))
        p_scr[h * R:(h + 1) * R, :] = p.astype(BF16)
    o = jnp.dot(p_scr[...], vv_ref[...].astype(BF16), preferred_element_type=F32)
    lo = lax.broadcasted_iota(jnp.int32, (R, LANES), 1) < HEAD_DIM
    for hp in range(N_HEADS // 2):
        even = o[(2 * hp) * R:(2 * hp + 1) * R] / denoms[2 * hp]
        odd = o[(2 * hp + 1) * R:(2 * hp + 2) * R] / denoms[2 * hp + 1]
        o_ref[:, hp * LANES:(hp + 1) * LANES] = jnp.where(lo, even, odd)


def _attention(P, kk, vv, ik, kv_blocks, *, topk, n_keys, causal):
    B, T, _ = P.shape
    L = kk.shape[1]
    nq = T // CHUNK
    dq = N_HEADS * HEAD_DIM
    kern = functools.partial(_attn_kernel, topk=topk, n_keys=n_keys, causal=causal)
    ck, cv, ci = kv_blocks
    return pl.pallas_call(
        kern,
        grid=(B, nq),
        in_specs=[
            pl.BlockSpec((None, CHUNK, dq), lambda b, j: (b, j, SEC_Q // dq)),
            pl.BlockSpec((None, CHUNK, dq), lambda b, j: (b, j, SEC_IQ // dq)),
            pl.BlockSpec((None, CHUNK, LANES), lambda b, j: (b, j, SEC_IW // LANES)),
            pl.BlockSpec((None, L, LANES), lambda b, j: (b, 0, ck)),
            pl.BlockSpec((None, L, LANES), lambda b, j: (b, 0, cv)),
            pl.BlockSpec((None, L, LANES), lambda b, j: (b, 0, ci)),
        ],
        out_specs=pl.BlockSpec((None, CHUNK, dq), lambda b, j: (b, j, 0)),
        out_shape=jax.ShapeDtypeStruct((B, T, dq), F32),
        scratch_shapes=[
            pltpu.VMEM((CHUNK, L), jnp.int32),
            pltpu.VMEM((CHUNK, L), F32),
            pltpu.VMEM((N_HEADS * CHUNK, L), BF16),
        ],
        compiler_params=_cparams("arbitrary", "arbitrary"),
        name="attn",
    )(P, P, P, kk, vv, ik)


def _conv_kernel(a_ref, g_ref, pa_ref, pg_ref, hc_ref, sb_ref, sc_ref, sx_ref, psc_ref, psx_ref,
                 hs_ref, wc_ref, bc_ref, lg_ref, lb_ref, ws_ref,
                 conf_ref, short_ref, stc_ref, sts_ref, u_scr, us_scr, sh_scr, *, rows):
    t = pl.program_id(1)
    TT = a_ref.shape[0]

    @pl.when(t == 0)
    def _():
        u_scr[0:HIST_CONF, :] = hc_ref[...]
        us_scr[0:HIST_SHORT, :] = hs_ref[...]

    @pl.when(t > 0)
    def _():
        u_scr[0:HIST_CONF, :] = pa_ref[...] * _sigmoid(pg_ref[...])
        us_scr[0:HIST_SHORT, :] = psc_ref[...] * psx_ref[...]

    u_scr[HIST_CONF:, :] = a_ref[...] * _sigmoid(g_ref[...])
    us_scr[HIST_SHORT:, :] = sc_ref[...] * sx_ref[...]

    stc_ref[...] = u_scr[TT:TT + HIST_CONF, :]
    sts_ref[...] = us_scr[TT:TT + HIST_SHORT, :]

    off_c = HIST_CONF - (CONF_WIDTH - 1)
    off_s = HIST_SHORT - (SC_WIDTH - 1)

    s = jnp.zeros((TT, D_SC), F32)
    for w in range(SC_WIDTH):
        s = s + us_scr[off_s + w:off_s + w + TT, :] * ws_ref[w:w + 1, :]
    short_ref[...] = (sb_ref[...] * s).astype(BF16)

    for ph in range(SUBLANES):
        span = TT + HIST_CONF - (SUBLANES if ph else 0)
        sh_scr[ph, 0:span, :] = u_scr[ph:ph + span, :]

    def body(r, carry):
        r0 = pl.multiple_of(r * rows, rows)
        acc = jnp.zeros((rows, D_CONF), F32) + bc_ref[...]
        for w in range(CONF_WIDTH):
            o = off_c + w
            tap = sh_scr[o % SUBLANES, pl.ds(r0 + (o // SUBLANES) * SUBLANES, rows), :]
            acc = acc + tap * wc_ref[w:w + 1, :]
        mu = jnp.mean(acc, axis=-1, keepdims=True)
        xc = acc - mu
        var = jnp.mean(xc * xc, axis=-1, keepdims=True)
        y = (xc * lax.rsqrt(var + EPS)) * lg_ref[...] + lb_ref[...]
        conf_ref[pl.ds(r0, rows), :] = _silu(y).astype(BF16)
        return carry

    lax.fori_loop(0, TT // rows, body, 0)


def _convs(P, hist_conf, hist_short, w_dw_conf, b_dw_conf, ln_g, ln_b, w_dw_short):
    B, T, _ = P.shape
    TT = min(T, 512)
    nt = T // TT
    rows = 32
    C = D_CONF

    def sec(off):
        return pl.BlockSpec((None, TT, C), lambda b, t: (b, t, off // C))

    def prev(off, h):
        return pl.BlockSpec((None, h, C), lambda b, t: (b, jnp.maximum(t * (TT // h) - 1, 0), off // C))

    def hist(h):
        return pl.BlockSpec((None, h, C), lambda b, t: (b, 0, 0))

    def vec(r):
        return pl.BlockSpec((r, C), lambda b, t: (0, 0))

    wc = jnp.zeros((32, C), F32).at[:CONF_WIDTH].set(w_dw_conf)
    ws = jnp.zeros((8, C), F32).at[:SC_WIDTH].set(w_dw_short)
    outs = pl.pallas_call(
        functools.partial(_conv_kernel, rows=rows),
        grid=(B, nt),
        in_specs=[
            sec(SEC_GLU_A), sec(SEC_GLU_G), prev(SEC_GLU_A, HIST_CONF), prev(SEC_GLU_G, HIST_CONF),
            hist(HIST_CONF),
            sec(SEC_SC_B), sec(SEC_SC_C), sec(SEC_SC_X), prev(SEC_SC_C, HIST_SHORT),
            prev(SEC_SC_X, HIST_SHORT), hist(HIST_SHORT),
            vec(32), vec(1), vec(1), vec(1), vec(8),
        ],
        out_specs=[
            pl.BlockSpec((None, TT, C), lambda b, t: (b, t, 0)),
            pl.BlockSpec((None, TT, C), lambda b, t: (b, t, 0)),
            pl.BlockSpec((None, HIST_CONF, C), lambda b, t: (b, 0, 0)),
            pl.BlockSpec((None, HIST_SHORT, C), lambda b, t: (b, 0, 0)),
        ],
        out_shape=[
            jax.ShapeDtypeStruct((B, T, C), BF16),
            jax.ShapeDtypeStruct((B, T, C), BF16),
            jax.ShapeDtypeStruct((B, HIST_CONF, C), F32),
            jax.ShapeDtypeStruct((B, HIST_SHORT, C), F32),
        ],
        scratch_shapes=[
            pltpu.VMEM((HIST_CONF + TT, C), F32),
            pltpu.VMEM((HIST_SHORT + TT, C), F32),
            pltpu.VMEM((SUBLANES, HIST_CONF + TT, C), F32),
        ],
        compiler_params=_cparams("arbitrary", "arbitrary"),
        name="convs",
    )(P, P, P, P, hist_conf, P, P, P, P, P, hist_short,
      wc, b_dw_conf.reshape(1, C), ln_g.reshape(1, C), ln_b.reshape(1, C), ws)
    return outs


def _merge_kernel(attn_ref, conf_ref, short_ref, ga_ref, gb_ref, gc_ref, x_ref, gm_ref,
                  wb_ref, wo_ref, o_ref):
    nb, tt, d = x_ref.shape
    n = nb * tt

    def branch(ref, i):
        v = ref[...].reshape(n, ref.shape[-1]).astype(BF16)
        return jnp.dot(v, wb_ref[i], preferred_element_type=F32)

    merged = (_sigmoid(ga_ref[...].reshape(n, d)) * branch(attn_ref, 0)
              + _sigmoid(gb_ref[...].reshape(n, d)) * branch(conf_ref, 1)
              + _sigmoid(gc_ref[...].reshape(n, d)) * branch(short_ref, 2))
    out = jnp.dot(merged.astype(BF16), wo_ref[...], preferred_element_type=F32)
    o_ref[...] = x_ref[...] + gm_ref[...] * out.reshape(nb, tt, d)


def _merge(attn, conf, short, P, x, mod, w_branch, w_out):
    B, T, D = x.shape
    nb, tt = _token_blocks(B, T, 512)
    C = attn.shape[-1]

    def tok(width, col):
        return pl.BlockSpec((nb, tt, width), lambda b, t: (b, t, col))

    g0 = SEC_GATES // D
    return pl.pallas_call(
        _merge_kernel,
        grid=(B // nb, T // tt),
        in_specs=[
            tok(C, 0), tok(C, 0), tok(C, 0),
            tok(D, g0), tok(D, g0 + 1), tok(D, g0 + 2),
            tok(D, 0),
            _mod_spec(nb, D, 2, lambda b, t: b),
            pl.BlockSpec((N_BRANCH, C, D), lambda b, t: (0, 0, 0)),
            pl.BlockSpec((D, D), lambda b, t: (0, 0)),
        ],
        out_specs=tok(D, 0),
        out_shape=jax.ShapeDtypeStruct((B, T, D), F32),
        compiler_params=_cparams("arbitrary", "arbitrary"),
        name="merge",
    )(attn, conf, short, P, P, P, x, mod, w_branch, w_out)


def _ffn_kernel(x_ref, g_ref, sc_ref, sh_ref, gf_ref, wg_ref, wu_ref, wo_ref, o_ref, h_scr, acc_scr):
    f = pl.program_id(2)

    @pl.when(f == 0)
    def _():
        h_scr[...] = _norm_mod(x_ref[...], g_ref[...], sc_ref[...], sh_ref[...])
        acc_scr[...] = jnp.zeros_like(acc_scr)

    h = h_scr[...]
    gate = jnp.dot(h, wg_ref[...], preferred_element_type=F32)
    up = jnp.dot(h, wu_ref[...], preferred_element_type=F32)
    act = (_silu(gate) * up).astype(BF16)
    acc_scr[...] += jnp.dot(act, wo_ref[...], preferred_element_type=F32)

    @pl.when(f == pl.num_programs(2) - 1)
    def _():
        nb, tt, d = x_ref.shape
        o_ref[...] = x_ref[...] + gf_ref[...] * acc_scr[...].reshape(nb, tt, d)


def _ffn(x, g, mod, w_in, w_out):
    B, T, D = x.shape
    nb, tt = _token_blocks(B, T)
    dff = w_out.shape[0]
    tf = dff // 2
    assert tf % LANES == 0
    nf = dff // tf
    return pl.pallas_call(
        _ffn_kernel,
        grid=(B // nb, T // tt, nf),
        in_specs=[
            pl.BlockSpec((nb, tt, D), lambda b, t, f: (b, t, 0)),
            pl.BlockSpec((1, D), lambda b, t, f: (0, 0)),
            _mod_spec(nb, D, 4, lambda b, t, f: b),
            _mod_spec(nb, D, 3, lambda b, t, f: b),
            _mod_spec(nb, D, 5, lambda b, t, f: b),
            pl.BlockSpec((D, tf), lambda b, t, f: (0, f)),
            pl.BlockSpec((D, tf), lambda b, t, f: (0, nf + f)),
            pl.BlockSpec((tf, D), lambda b, t, f: (f, 0)),
        ],
        out_specs=pl.BlockSpec((nb, tt, D), lambda b, t, f: (b, t, 0)),
        out_shape=jax.ShapeDtypeStruct((B, T, D), F32),
        scratch_shapes=[pltpu.VMEM((nb * tt, D), BF16), pltpu.VMEM((nb * tt, D), F32)],
        compiler_params=_cparams("arbitrary", "arbitrary", "arbitrary"),
        name="ffn",
    )(x, g, mod, mod, mod, w_in, w_in, w_out)


def _moe_kernel(x_ref, g_ref, sc_ref, sh_ref, gf_ref, wr_ref, br_ref, wg_ref, wu_ref, wo_ref,
                o_ref, h_scr, comb_scr, acc_scr):
    e = pl.program_id(2)
    lane = lax.broadcasted_iota(jnp.int32, comb_scr.shape, 1)

    @pl.when(e == 0)
    def _():
        h = _norm_mod(x_ref[...], g_ref[...], sc_ref[...], sh_ref[...])
        h_scr[...] = h
        acc_scr[...] = jnp.zeros_like(acc_scr)
        logits = jnp.dot(h, wr_ref[...], preferred_element_type=F32) + br_ref[...]
        logits = jnp.where(lane < N_EXPERTS, logits, -jnp.inf)
        m1 = jnp.max(logits, axis=1, keepdims=True)
        i1 = jnp.min(jnp.where(logits == m1, lane, LANES), axis=1, keepdims=True)
        rest = jnp.where(lane == i1, -jnp.inf, logits)
        m2 = jnp.max(rest, axis=1, keepdims=True)
        i2 = jnp.min(jnp.where(rest == m2, lane, LANES), axis=1, keepdims=True)
        e2 = jnp.exp(m2 - m1)
        w1 = 1.0 / (1.0 + e2)
        w2 = e2 / (1.0 + e2)
        comb_scr[...] = jnp.where(lane == i1, w1, 0.0) + jnp.where(lane == i2, w2, 0.0)

    h = h_scr[...]
    gate = jnp.dot(h, wg_ref[...], preferred_element_type=F32)
    up = jnp.dot(h, wu_ref[...], preferred_element_type=F32)
    act = (_silu(gate) * up).astype(BF16)
    y = jnp.dot(act, wo_ref[...], preferred_element_type=F32)
    c_e = jnp.sum(jnp.where(lane == e, comb_scr[...], 0.0), axis=1, keepdims=True)
    acc_scr[...] += c_e * y

    @pl.when(e == pl.num_programs(2) - 1)
    def _():
        nb, tt, d = x_ref.shape
        o_ref[...] = x_ref[...] + gf_ref[...] * acc_scr[...].reshape(nb, tt, d)


def _moe(x, g, mod, w_router, b_router, w_exp_in, w_exp_out):
    B, T, D = x.shape
    nb, tt = _token_blocks(B, T)
    ne, dfe, _ = w_exp_out.shape
    assert ne == N_EXPERTS and dfe % LANES == 0
    wr = jnp.zeros((D, LANES), BF16).at[:, :ne].set(w_router.astype(BF16))
    br = jnp.zeros((1, LANES), F32).at[0, :ne].set(b_router)
    return pl.pallas_call(
        _moe_kernel,
        grid=(B // nb, T // tt, ne),
        in_specs=[
            pl.BlockSpec((nb, tt, D), lambda b, t, e: (b, t, 0)),
            pl.BlockSpec((1, D), lambda b, t, e: (0, 0)),
            _mod_spec(nb, D, 4, lambda b, t, e: b),
            _mod_spec(nb, D, 3, lambda b, t, e: b),
            _mod_spec(nb, D, 5, lambda b, t, e: b),
            pl.BlockSpec((D, LANES), lambda b, t, e: (0, 0)),
            pl.BlockSpec((1, LANES), lambda b, t, e: (0, 0)),
            pl.BlockSpec((None, D, dfe), lambda b, t, e: (e, 0, 0)),
            pl.BlockSpec((None, D, dfe), lambda b, t, e: (e, 0, 1)),
            pl.BlockSpec((None, dfe, D), lambda b, t, e: (e, 0, 0)),
        ],
        out_specs=pl.BlockSpec((nb, tt, D), lambda b, t, e: (b, t, 0)),
        out_shape=jax.ShapeDtypeStruct((B, T, D), F32),
        scratch_shapes=[pltpu.VMEM((nb * tt, D), BF16), pltpu.VMEM((nb * tt, LANES), F32),
                        pltpu.VMEM((nb * tt, D), F32)],
        compiler_params=_cparams("arbitrary", "arbitrary", "arbitrary"),
        name="moe",
    )(x, g, mod, mod, mod, wr, br, w_exp_in, w_exp_in, w_exp_out)


def _final_kernel(x_ref, g_ref, o_ref):
    x = x_ref[...]
    ms = jnp.mean(x * x, axis=-1, keepdims=True)
    o_ref[...] = (x * lax.rsqrt(ms + EPS)) * g_ref[...]


def _final_norm(x, g):
    B, T, D = x.shape
    nb, tt = _token_blocks(B, T)
    return pl.pallas_call(
        _final_kernel,
        grid=(B // nb, T // tt),
        in_specs=[pl.BlockSpec((nb, tt, D), lambda b, t: (b, t, 0)),
                  pl.BlockSpec((1, D), lambda b, t: (0, 0))],
        out_specs=pl.BlockSpec((nb, tt, D), lambda b, t: (b, t, 0)),
        out_shape=jax.ShapeDtypeStruct((B, T, D), F32),
        compiler_params=_cparams("arbitrary", "arbitrary"),
        name="final_norm",
    )(x, g.reshape(1, D))


def _dup(a):
    return jnp.concatenate([a, a], axis=-1)


def _pad_rows(a, rows):
    return jnp.pad(a, ((0, 0), (0, rows - a.shape[1]), (0, 0)))


def _trunk(x, mods, past, p):
    B, T, D = x.shape
    depth = p["w_in"].shape[0]
    states = [[], [], [], [], []]
    for l in range(depth):
        mod = mods[l]
        P = _inproj(x, p["g_norm_mix"][l].reshape(1, D), mod, p["w_in"][l])
        k_new = P[:, :, SEC_KK:SEC_KK + HEAD_DIM]
        v_new = P[:, :, SEC_VV:SEC_VV + HEAD_DIM]
        ik_new = P[:, :, SEC_IK:SEC_IK + IDX_DIM]
        if past is None:
            attn = _attention(P, P, P, P, (SEC_KK // LANES, SEC_VV // LANES, SEC_IK // LANES),
                              topk=min(TOPK_MAX, T // 4), n_keys=T, causal=True)
            hist_conf = jnp.zeros((B, HIST_CONF, D_CONF), F32)
            hist_short = jnp.zeros((B, HIST_SHORT, D_SC), F32)
        else:
            cache_k, cache_v, cache_ik, conf_hist, short_hist = [a[l] for a in past]
            n_keys = cache_k.shape[1] + T
            lpad = -(-n_keys // LANES) * LANES
            kk = _pad_rows(_dup(jnp.concatenate([cache_k, k_new], axis=1)), lpad)
            vv = _pad_rows(_dup(jnp.concatenate([cache_v, v_new], axis=1)), lpad)
            ik = _pad_rows(_dup(jnp.concatenate([cache_ik, ik_new], axis=1)), lpad)
            attn = _attention(P, kk, vv, ik, (0, 0, 0),
                              topk=min(TOPK_MAX, n_keys // 4), n_keys=n_keys, causal=False)
            hist_conf = jnp.pad(conf_hist, ((0, 0), (HIST_CONF - (CONF_WIDTH - 1), 0), (0, 0)))
            hist_short = jnp.pad(short_hist, ((0, 0), (HIST_SHORT - (SC_WIDTH - 1), 0), (0, 0)))
        conf, short, st_conf, st_short = _convs(
            P, hist_conf, hist_short, p["w_dw_conf"][l], p["b_dw_conf"][l], p["ln_conf_g"][l],
            p["ln_conf_b"][l], p["w_dw_short"][l])
        x = _merge(attn, conf, short, P, x, mod, p["w_branch"][l], p["w_out"][l])
        if l % 2 == 0:
            x = _ffn(x, p["g_norm_ffn"][l].reshape(1, D), mod, p["w_ffn_in"][l // 2], p["w_ffn_out"][l // 2])
        else:
            x = _moe(x, p["g_norm_ffn"][l].reshape(1, D), mod, p["w_router"][l // 2], p["b_router"][l // 2],
                     p["w_exp_in"][l // 2], p["w_exp_out"][l // 2])
        for i, s in enumerate((k_new, v_new, ik_new, st_conf[:, HIST_CONF - (CONF_WIDTH - 1):],
                               st_short[:, HIST_SHORT - (SC_WIDTH - 1):])):
            states[i].append(s)
    y = _final_norm(x, p["g_final"])
    return y, tuple(jnp.stack(s) for s in states)


def kernel(x_prompt, x_sample, cache_k, cache_v, cache_idx_k, state_conv_conformer, state_conv_short,
           c_prompt, c_sample, w_ada, b_ada, g_norm_mix, g_norm_ffn, w_in, w_dw_conf, b_dw_conf,
           ln_conf_g, ln_conf_b, w_dw_short, w_branch, w_out, w_ffn_in, w_ffn_out, w_router, b_router,
           w_exp_in, w_exp_out, g_final):
    depth, D, _ = w_in.shape
    bp = x_prompt.shape[0]
    mods = _ada(jnp.concatenate([c_prompt, c_sample], axis=0), w_ada, b_ada)
    mods = mods.reshape(depth, -1, 6, 1, D)
    params = {
        "g_norm_mix": g_norm_mix, "g_norm_ffn": g_norm_ffn,
        "w_in": jnp.stack([_rearrange_w_in(w_in[l]) for l in range(depth)]),
        "w_dw_conf": w_dw_conf, "b_dw_conf": b_dw_conf, "ln_conf_g": ln_conf_g, "ln_conf_b": ln_conf_b,
        "w_dw_short": w_dw_short, "w_branch": w_branch.astype(BF16), "w_out": w_out.astype(BF16),
        "w_ffn_in": w_ffn_in.astype(BF16), "w_ffn_out": w_ffn_out.astype(BF16),
        "w_router": w_router, "b_router": b_router,
        "w_exp_in": w_exp_in.astype(BF16), "w_exp_out": w_exp_out.astype(BF16), "g_final": g_final,
    }
    y_p, (pk, pv, pik, pconf, pshort) = _trunk(x_prompt, mods[:, :bp], None, params)
    y_s, (sk, sv, sik, sconf, sshort) = _trunk(
        x_sample, mods[:, bp:],
        (cache_k, cache_v, cache_idx_k, state_conv_conformer, state_conv_short), params)
    return (y_p, y_s, pk, pv, pik, pconf, pshort, sk, sv, sik, sconf, sshort)
```

```python
import functools

import jax
import jax.numpy as jnp
import numpy as np
from jax import lax
from jax.experimental import pallas as pl
from jax.experimental.pallas import tpu as pltpu

CHUNK = 64
N_HEADS = 8
HEAD_DIM = 64
IDX_HEADS = 8
IDX_DIM = 64
TOPK_MAX = 256
D_CONF = 512
CONF_WIDTH = 31
D_SC = 512
SC_WIDTH = 3
N_BRANCH = 3
N_EXPERTS = 8
EPS = 1e-6

LANES = 128
SUBLANES = 8
VMEM_LIMIT = 56 * 1024 * 1024
INT_MIN = np.int32(-2 ** 31)

F32 = jnp.float32
BF16 = jnp.bfloat16

SEC_Q = 0
SEC_IQ = 512
SEC_KK = 1024
SEC_VV = 1152
SEC_IK = 1280
SEC_IW = 1408
SEC_GLU_A = 1536
SEC_GLU_G = 2048
SEC_SC_B = 2560
SEC_SC_C = 3072
SEC_SC_X = 3584
SEC_GATES = 4096
P_WIDTH = 7168
HIST_CONF = 32
HIST_SHORT = 8


def _cparams(*sem):
    return pltpu.CompilerParams(dimension_semantics=sem, vmem_limit_bytes=VMEM_LIMIT)


def _sigmoid(x):
    return 1.0 / (1.0 + jnp.exp(-x))


def _silu(x):
    return x * _sigmoid(x)


def _token_blocks(B, T, rows=1024):
    if T >= rows:
        assert T % rows == 0
        return 1, rows
    nb = min(B, rows // T)
    assert B % nb == 0
    return nb, T


def _ada_kernel(c_ref, w_ref, b_ref, o_ref):
    c = _silu(c_ref[...]).astype(BF16)
    o_ref[...] = jnp.dot(c, w_ref[...].astype(BF16), preferred_element_type=F32) + b_ref[...]


def _ada(c, w_ada, b_ada):
    depth, d, n = w_ada.shape
    bc = c.shape[0]
    tn = 1536
    assert n % tn == 0
    return pl.pallas_call(
        _ada_kernel,
        grid=(depth, n // tn),
        in_specs=[
            pl.BlockSpec((bc, d), lambda l, j: (0, 0)),
            pl.BlockSpec((None, d, tn), lambda l, j: (l, 0, j)),
            pl.BlockSpec((None, 1, tn), lambda l, j: (l, 0, j)),
        ],
        out_specs=pl.BlockSpec((None, bc, tn), lambda l, j: (l, 0, j)),
        out_shape=jax.ShapeDtypeStruct((depth, bc, n), F32),
        compiler_params=_cparams("arbitrary", "arbitrary"),
        name="ada",
    )(c, w_ada, b_ada.reshape(depth, 1, n))


def _norm_mod(x, g, scale, shift):
    ms = jnp.mean(x * x, axis=-1, keepdims=True)
    xn = (x * lax.rsqrt(ms + EPS)) * g
    h = xn * (1.0 + scale) + shift
    nb, tt, d = x.shape
    return h.reshape(nb * tt, d).astype(BF16)


def _mod_spec(nb, d, which, bmap):
    return pl.BlockSpec((nb, None, 1, d), lambda *a: (bmap(*a), which, 0, 0))


def _inproj_kernel(x_ref, g_ref, sc_ref, sh_ref, w_ref, o_ref, h_scr):
    @pl.when(pl.program_id(2) == 0)
    def _():
        h_scr[...] = _norm_mod(x_ref[...], g_ref[...], sc_ref[...], sh_ref[...])

    nb, tt, tn = o_ref.shape
    o_ref[...] = jnp.dot(h_scr[...], w_ref[...], preferred_element_type=F32).reshape(nb, tt, tn)


def _inproj(x, g, mod, w):
    B, T, D = x.shape
    nb, tt = _token_blocks(B, T)
    tn = 1024
    width = w.shape[1]
    assert width % tn == 0
    grid = (B // nb, T // tt, width // tn)
    return pl.pallas_call(
        _inproj_kernel,
        grid=grid,
        in_specs=[
            pl.BlockSpec((nb, tt, D), lambda b, t, j: (b, t, 0)),
            pl.BlockSpec((1, D), lambda b, t, j: (0, 0)),
            _mod_spec(nb, D, 1, lambda b, t, j: b),
            _mod_spec(nb, D, 0, lambda b, t, j: b),
            pl.BlockSpec((D, tn), lambda b, t, j: (0, j)),
        ],
        out_specs=pl.BlockSpec((nb, tt, tn), lambda b, t, j: (b, t, j)),
        out_shape=jax.ShapeDtypeStruct((B, T, width), F32),
        scratch_shapes=[pltpu.VMEM((nb * tt, D), BF16)],
        compiler_params=_cparams("arbitrary", "arbitrary", "arbitrary"),
        name="inproj",
    )(x, g, mod, mod, w)


def _rearrange_w_in(w_in):
    d = w_in.shape[0]
    sizes = (N_HEADS * HEAD_DIM, HEAD_DIM, HEAD_DIM, IDX_HEADS * IDX_DIM, IDX_DIM, IDX_HEADS,
             2 * D_CONF, 3 * D_SC, N_BRANCH * d)
    offs = np.cumsum((0,) + sizes)
    q, k, v, iq, ik, iw, glu, sc, gates = [w_in[:, offs[i]:offs[i + 1]] for i in range(9)]
    iw_pad = jnp.zeros((d, LANES - IDX_HEADS), w_in.dtype)
    out = jnp.concatenate([q, iq, k, k, v, v, ik, ik, iw, iw_pad, glu, sc, gates], axis=1)
    assert out.shape[1] == P_WIDTH
    return out.astype(BF16)


def _stack_heads(x):
    rows = x.shape[0]
    lane = lax.broadcasted_iota(jnp.int32, (rows, LANES), 1)
    lo = lane < HEAD_DIM
    parts = []
    for h in range(N_HEADS):
        slab = x[:, (h // 2) * LANES:(h // 2 + 1) * LANES]
        parts.append(jnp.where(lo if h % 2 == 0 else jnp.logical_not(lo), slab, 0.0))
    return jnp.concatenate(parts, axis=0).astype(BF16)


def _count(mask):
    return jnp.sum(jnp.where(mask, 1.0, 0.0), axis=1, keepdims=True)


def _attn_kernel(q_ref, iq_ref, iw_ref, kk_ref, vv_ref, ik_ref, o_ref, key_scr, sel_scr, p_scr,
                 *, topk, n_keys, causal):
    R = CHUNK
    L = kk_ref.shape[0]
    nt = (((1,), (1,)), ((), ()))
    if causal:
        limit = (pl.program_id(1) + 1) * CHUNK
    else:
        limit = n_keys
    pos = lax.broadcasted_iota(jnp.int32, (R, L), 1)
    admissible = pos < limit

    rel = lax.dot_general(_stack_heads(iq_ref[...]), ik_ref[...].astype(BF16), nt,
                          preferred_element_type=F32)
    iw = iw_ref[...]
    score = jnp.zeros((R, L), F32)
    for h in range(IDX_HEADS):
        r = jnp.maximum(rel[h * R:(h + 1) * R] * (IDX_DIM ** -0.5), 0.0)
        score = score + (iw[:, h:h + 1] * (IDX_HEADS ** -0.5)) * r

    bits = lax.bitcast_convert_type(score, jnp.int32)
    key = jnp.where(bits < 0, bits ^ jnp.int32(0x7FFFFFFF), bits)
    key_scr[...] = jnp.where(admissible, key, INT_MIN)

    def search(i, thr):
        cand = thr + lax.shift_left(jnp.int32(1), 31 - i)
        cnt = _count(key_scr[...] >= cand)
        return jnp.where(cnt >= topk, cand, thr)

    thr = lax.fori_loop(0, 32, search, jnp.full((R, 1), INT_MIN, jnp.int32))
    keys = key_scr[...]
    n_ge = _count(keys >= thr)
    has_tie = jnp.max(n_ge) > topk

    @pl.when(jnp.logical_not(has_tie))
    def _():
        sel_scr[...] = jnp.where(jnp.logical_and(key_scr[...] >= thr, admissible), 1.0, 0.0)

    @pl.when(has_tie)
    def _():
        kk_ = key_scr[...]
        eq = kk_ == thr
        need = topk - _count(kk_ > thr)
        nbits = int(L - 1).bit_length()

        def tie_search(i, d):
            cand = d + lax.shift_left(jnp.int32(1), nbits - 1 - i)
            f = _count(jnp.logical_and(eq, pos < cand))
            return jnp.where(f < need, cand, d)

        d = lax.fori_loop(0, nbits, tie_search, jnp.zeros((R, 1), jnp.int32))
        chosen = jnp.logical_or(kk_ > thr, jnp.logical_and(eq, pos <= d))
        sel_scr[...] = jnp.where(jnp.logical_and(chosen, admissible), 1.0, 0.0)

    logits = lax.dot_general(_stack_heads(q_ref[...]), kk_ref[...].astype(BF16), nt,
                             preferred_element_type=F32) * (HEAD_DIM ** -0.5)
    sel = sel_scr[...] > 0.5
    denoms = []
    for h in range(N_HEADS):
        lg = jnp.where(sel, logits[h * R:(h + 1) * R], -jnp.inf)
        m = jnp.max(lg, axis=1, keepdims=True)
        p = jnp.exp(lg - m)
        denoms.append(jnp.sum(p, axis=1, keepdims=True))
        p_scr[h * R:(h + 1) * R, :] = p.astype(BF16)
    o = jnp.dot(p_scr[...], vv_ref[...].astype(BF16), preferred_element_type=F32)
    lo = lax.broadcasted_iota(jnp.int32, (R, LANES), 1) < HEAD_DIM
    for hp in range(N_HEADS // 2):
        even = o[(2 * hp) * R:(2 * hp + 1) * R] / denoms[2 * hp]
        odd = o[(2 * hp + 1) * R:(2 * hp + 2) * R] / denoms[2 * hp + 1]
        o_ref[:, hp * LANES:(hp + 1) * LANES] = jnp.where(lo, even, odd)


def _attention(P, kk, vv, ik, kv_blocks, *, topk, n_keys, causal):
    B, T, _ = P.shape
    L = kk.shape[1]
    nq = T // CHUNK
    dq = N_HEADS * HEAD_DIM
    kern = functools.partial(_attn_kernel, topk=topk, n_keys=n_keys, causal=causal)
    ck, cv, ci = kv_blocks
    return pl.pallas_call(
        kern,
        grid=(B, nq),
        in_specs=[
            pl.BlockSpec((None, CHUNK, dq), lambda b, j: (b, j, SEC_Q // dq)),
            pl.BlockSpec((None, CHUNK, dq), lambda b, j: (b, j, SEC_IQ // dq)),
            pl.BlockSpec((None, CHUNK, LANES), lambda b, j: (b, j, SEC_IW // LANES)),
            pl.BlockSpec((None, L, LANES), lambda b, j: (b, 0, ck)),
            pl.BlockSpec((None, L, LANES), lambda b, j: (b, 0, cv)),
            pl.BlockSpec((None, L, LANES), lambda b, j: (b, 0, ci)),
        ],
        out_specs=pl.BlockSpec((None, CHUNK, dq), lambda b, j: (b, j, 0)),
        out_shape=jax.ShapeDtypeStruct((B, T, dq), F32),
        scratch_shapes=[
            pltpu.VMEM((CHUNK, L), jnp.int32),
            pltpu.VMEM((CHUNK, L), F32),
            pltpu.VMEM((N_HEADS * CHUNK, L), BF16),
        ],
        compiler_params=_cparams("arbitrary", "arbitrary"),
        name="attn",
    )(P, P, P, kk, vv, ik)


QT = 2 * CHUNK
KB = 256


SUM_ROWS = 64


def _colcount(n_rows, pred):
    acc = jnp.zeros((SUM_ROWS, LANES), F32)
    for k in range(n_rows // SUM_ROWS):
        acc = acc + jnp.where(pred(slice(k * SUM_ROWS, (k + 1) * SUM_ROWS)), 1.0, 0.0)
    return jnp.sum(acc, axis=0, keepdims=True)


def _attn_t_kernel(q_ref, iq_ref, iw_ref, kk_ref, vv_ref, ik_ref, o_ref,
                   key_scr, sel_scr, lg_scr, p_scr, vve_scr, *, topk, tile0):
    L = kk_ref.shape[0]
    nkb = L // KB
    nh = N_HEADS
    nt = (((1,), (1,)), ((), ()))
    tn = (((0,), (0,)), ((), ()))
    tile = tile0 + pl.program_id(1)
    lane = lax.broadcasted_iota(jnp.int32, (1, LANES), 1)
    limit = tile * QT + jnp.where(lane < CHUNK, CHUNK, QT)

    w_t = jnp.transpose(iw_ref[...])
    w_rows = [(w_t[h:h + 1, :] * (IDX_HEADS ** -0.5)) * (IDX_DIM ** -0.5) for h in range(nh)]
    qs_i = _stack_heads(iq_ref[...])
    for kb in range(nkb):
        rows = slice(kb * KB, (kb + 1) * KB)
        rel = lax.dot_general(ik_ref[rows, :].astype(BF16), qs_i, nt, preferred_element_type=F32)
        score = jnp.zeros((KB, LANES), F32)
        for h in range(nh):
            score = score + w_rows[h] * jnp.maximum(rel[:, h * QT:(h + 1) * QT], 0.0)
        bits = lax.bitcast_convert_type(score, jnp.int32)
        key = jnp.where(bits < 0, bits ^ jnp.int32(0x7FFFFFFF), bits)
        pos = kb * KB + lax.broadcasted_iota(jnp.int32, (KB, LANES), 0)
        key_scr[rows, :] = jnp.where(pos < limit, key, INT_MIN)
        vve_scr[rows, 0:LANES] = vv_ref[rows, :].astype(BF16)
        vve_scr[rows, LANES:2 * LANES] = jnp.ones((KB, LANES), BF16)

    def search(i, thr):
        cand = thr + lax.shift_left(jnp.int32(1), 31 - i)
        cnt = _colcount(L, lambda r: key_scr[r, :] >= cand)
        return jnp.where(cnt >= topk, cand, thr)

    thr = lax.fori_loop(0, 32, search, jnp.full((1, LANES), INT_MIN, jnp.int32))
    has_tie = jnp.max(_colcount(L, lambda r: key_scr[r, :] >= thr)) > topk

    def row_pos(r):
        return r.start + lax.broadcasted_iota(jnp.int32, (r.stop - r.start, LANES), 0)

    @pl.when(jnp.logical_not(has_tie))
    def _():
        for kb in range(nkb):
            r = slice(kb * KB, (kb + 1) * KB)
            chosen = jnp.logical_and(key_scr[r, :] >= thr, row_pos(r) < limit)
            sel_scr[r, :] = jnp.where(chosen, 1.0, 0.0)

    @pl.when(has_tie)
    def _():
        need = topk - _colcount(L, lambda r: key_scr[r, :] > thr)
        nbits = int(L - 1).bit_length()

        def tie_search(i, d):
            cand = d + lax.shift_left(jnp.int32(1), nbits - 1 - i)
            f = _colcount(L, lambda r: jnp.logical_and(key_scr[r, :] == thr, row_pos(r) < cand))
            return jnp.where(f < need, cand, d)

        d = lax.fori_loop(0, nbits, tie_search, jnp.zeros((1, LANES), jnp.int32))
        for kb in range(nkb):
            r = slice(kb * KB, (kb + 1) * KB)
            keys = key_scr[r, :]
            pos = row_pos(r)
            chosen = jnp.logical_or(keys > thr, jnp.logical_and(keys == thr, pos <= d))
            sel_scr[r, :] = jnp.where(jnp.logical_and(chosen, pos < limit), 1.0, 0.0)

    qs = _stack_heads(q_ref[...] * (HEAD_DIM ** -0.5))
    m_run = [jnp.full((1, QT), -jnp.inf, F32) for _ in range(nh)]
    for kb in range(nkb):
        rows = slice(kb * KB, (kb + 1) * KB)
        lg = lax.dot_general(kk_ref[rows, :].astype(BF16), qs, nt, preferred_element_type=F32)
        sel = sel_scr[rows, :] > 0.5
        for h in range(nh):
            lgm = jnp.where(sel, lg[:, h * QT:(h + 1) * QT], -jnp.inf)
            lg_scr[rows, h * QT:(h + 1) * QT] = lgm
            m_run[h] = jnp.maximum(m_run[h], jnp.max(lgm, axis=0, keepdims=True))
    m_all = jnp.concatenate(m_run, axis=1)

    acc = jnp.zeros((nh * QT, 2 * LANES), F32)
    for kb in range(nkb):
        rows = slice(kb * KB, (kb + 1) * KB)
        p_scr[rows, :] = jnp.exp(lg_scr[rows, :] - m_all).astype(BF16)
        acc = acc + lax.dot_general(p_scr[rows, :], vve_scr[rows, :], tn, preferred_element_type=F32)
    on = acc[:, 0:LANES] / acc[:, LANES:2 * LANES]
    lo = lax.broadcasted_iota(jnp.int32, (QT, LANES), 1) < HEAD_DIM
    for hp in range(nh // 2):
        even = on[(2 * hp) * QT:(2 * hp + 1) * QT]
        odd = on[(2 * hp + 1) * QT:(2 * hp + 2) * QT]
        o_ref[:, hp * LANES:(hp + 1) * LANES] = jnp.where(lo, even, odd)


def _attention_prompt(P, *, topk):
    B, T, _ = P.shape
    dq = N_HEADS * HEAD_DIM
    assert T % KB == 0 and QT == LANES
    tiles_per_class = KB // QT
    ck, cv, ci = SEC_KK // LANES, SEC_VV // LANES, SEC_IK // LANES
    outs = []
    for c in range(T // KB):
        L = KB * (c + 1)
        t0 = c * tiles_per_class

        def qspec(width, col, t0=t0):
            return pl.BlockSpec((None, QT, width), lambda b, i: (b, t0 + i, col))

        def kspec(col, L=L):
            return pl.BlockSpec((None, L, LANES), lambda b, i: (b, 0, col))

        outs.append(pl.pallas_call(
            functools.partial(_attn_t_kernel, topk=topk, tile0=t0),
            grid=(B, tiles_per_class),
            in_specs=[qspec(dq, SEC_Q // dq), qspec(dq, SEC_IQ // dq), qspec(LANES, SEC_IW // LANES),
                      kspec(ck), kspec(cv), kspec(ci)],
            out_specs=pl.BlockSpec((None, QT, dq), lambda b, i: (b, i, 0)),
            out_shape=jax.ShapeDtypeStruct((B, tiles_per_class * QT, dq), F32),
            scratch_shapes=[
                pltpu.VMEM((L, LANES), jnp.int32),
                pltpu.VMEM((L, LANES), F32),
                pltpu.VMEM((L, N_HEADS * QT), F32),
                pltpu.VMEM((L, N_HEADS * QT), BF16),
                pltpu.VMEM((L, 2 * LANES), BF16),
            ],
            compiler_params=_cparams("arbitrary", "arbitrary"),
            name=f"attn_prompt_L{L}",
        )(P, P, P, P, P, P))
    return jnp.concatenate(outs, axis=1)


def _conv_kernel(a_ref, g_ref, pa_ref, pg_ref, hc_ref, sb_ref, sc_ref, sx_ref, psc_ref, psx_ref,
                 hs_ref, wc_ref, bc_ref, lg_ref, lb_ref, ws_ref,
                 conf_ref, short_ref, stc_ref, sts_ref, u_scr, us_scr, sh_scr, *, rows):
    t = pl.program_id(1)
    TT = a_ref.shape[0]

    @pl.when(t == 0)
    def _():
        u_scr[0:HIST_CONF, :] = hc_ref[...]
        us_scr[0:HIST_SHORT, :] = hs_ref[...]

    @pl.when(t > 0)
    def _():
        u_scr[0:HIST_CONF, :] = pa_ref[...] * _sigmoid(pg_ref[...])
        us_scr[0:HIST_SHORT, :] = psc_ref[...] * psx_ref[...]

    u_scr[HIST_CONF:, :] = a_ref[...] * _sigmoid(g_ref[...])
    us_scr[HIST_SHORT:, :] = sc_ref[...] * sx_ref[...]

    stc_ref[...] = u_scr[TT:TT + HIST_CONF, :]
    sts_ref[...] = us_scr[TT:TT + HIST_SHORT, :]

    off_c = HIST_CONF - (CONF_WIDTH - 1)
    off_s = HIST_SHORT - (SC_WIDTH - 1)

    s = jnp.zeros((TT, D_SC), F32)
    for w in range(SC_WIDTH):
        s = s + us_scr[off_s + w:off_s + w + TT, :] * ws_ref[w:w + 1, :]
    short_ref[...] = (sb_ref[...] * s).astype(BF16)

    for ph in range(SUBLANES):
        span = TT + HIST_CONF - (SUBLANES if ph else 0)
        sh_scr[ph, 0:span, :] = u_scr[ph:ph + span, :]

    def body(r, carry):
        r0 = pl.multiple_of(r * rows, rows)
        acc = jnp.zeros((rows, D_CONF), F32) + bc_ref[...]
        for w in range(CONF_WIDTH):
            o = off_c + w
            tap = sh_scr[o % SUBLANES, pl.ds(r0 + (o // SUBLANES) * SUBLANES, rows), :]
            acc = acc + tap * wc_ref[w:w + 1, :]
        mu = jnp.mean(acc, axis=-1, keepdims=True)
        xc = acc - mu
        var = jnp.mean(xc * xc, axis=-1, keepdims=True)
        y = (xc * lax.rsqrt(var + EPS)) * lg_ref[...] + lb_ref[...]
        conf_ref[pl.ds(r0, rows), :] = _silu(y).astype(BF16)
        return carry

    lax.fori_loop(0, TT // rows, body, 0)


def _convs(P, hist_conf, hist_short, w_dw_conf, b_dw_conf, ln_g, ln_b, w_dw_short):
    B, T, _ = P.shape
    TT = min(T, 512)
    nt = T // TT
    rows = 32
    C = D_CONF

    def sec(off):
        return pl.BlockSpec((None, TT, C), lambda b, t: (b, t, off // C))

    def prev(off, h):
        return pl.BlockSpec((None, h, C), lambda b, t: (b, jnp.maximum(t * (TT // h) - 1, 0), off // C))

    def hist(h):
        return pl.BlockSpec((None, h, C), lambda b, t: (b, 0, 0))

    def vec(r):
        return pl.BlockSpec((r, C), lambda b, t: (0, 0))

    wc = jnp.zeros((32, C), F32).at[:CONF_WIDTH].set(w_dw_conf)
    ws = jnp.zeros((8, C), F32).at[:SC_WIDTH].set(w_dw_short)
    outs = pl.pallas_call(
        functools.partial(_conv_kernel, rows=rows),
        grid=(B, nt),
        in_specs=[
            sec(SEC_GLU_A), sec(SEC_GLU_G), prev(SEC_GLU_A, HIST_CONF), prev(SEC_GLU_G, HIST_CONF),
            hist(HIST_CONF),
            sec(SEC_SC_B), sec(SEC_SC_C), sec(SEC_SC_X), prev(SEC_SC_C, HIST_SHORT),
            prev(SEC_SC_X, HIST_SHORT), hist(HIST_SHORT),
            vec(32), vec(1), vec(1), vec(1), vec(8),
        ],
        out_specs=[
            pl.BlockSpec((None, TT, C), lambda b, t: (b, t, 0)),
            pl.BlockSpec((None, TT, C), lambda b, t: (b, t, 0)),
            pl.BlockSpec((None, HIST_CONF, C), lambda b, t: (b, 0, 0)),
            pl.BlockSpec((None, HIST_SHORT, C), lambda b, t: (b, 0, 0)),
        ],
        out_shape=[
            jax.ShapeDtypeStruct((B, T, C), BF16),
            jax.ShapeDtypeStruct((B, T, C), BF16),
            jax.ShapeDtypeStruct((B, HIST_CONF, C), F32),
            jax.ShapeDtypeStruct((B, HIST_SHORT, C), F32),
        ],
        scratch_shapes=[
            pltpu.VMEM((HIST_CONF + TT, C), F32),
            pltpu.VMEM((HIST_SHORT + TT, C), F32),
            pltpu.VMEM((SUBLANES, HIST_CONF + TT, C), F32),
        ],
        compiler_params=_cparams("arbitrary", "arbitrary"),
        name="convs",
    )(P, P, P, P, hist_conf, P, P, P, P, P, hist_short,
      wc, b_dw_conf.reshape(1, C), ln_g.reshape(1, C), ln_b.reshape(1, C), ws)
    return outs


def _merge_kernel(attn_ref, conf_ref, short_ref, ga_ref, gb_ref, gc_ref, x_ref, gm_ref,
                  wb_ref, wo_ref, o_ref):
    nb, tt, d = x_ref.shape
    n = nb * tt

    def branch(ref, i):
        v = ref[...].reshape(n, ref.shape[-1]).astype(BF16)
        return jnp.dot(v, wb_ref[i], preferred_element_type=F32)

    merged = (_sigmoid(ga_ref[...].reshape(n, d)) * branch(attn_ref, 0)
              + _sigmoid(gb_ref[...].reshape(n, d)) * branch(conf_ref, 1)
              + _sigmoid(gc_ref[...].reshape(n, d)) * branch(short_ref, 2))
    out = jnp.dot(merged.astype(BF16), wo_ref[...], preferred_element_type=F32)
    o_ref[...] = x_ref[...] + gm_ref[...] * out.reshape(nb, tt, d)


def _merge(attn, conf, short, P, x, mod, w_branch, w_out):
    B, T, D = x.shape
    nb, tt = _token_blocks(B, T, 512)
    C = attn.shape[-1]

    def tok(width, col):
        return pl.BlockSpec((nb, tt, width), lambda b, t: (b, t, col))

    g0 = SEC_GATES // D
    return pl.pallas_call(
        _merge_kernel,
        grid=(B // nb, T // tt),
        in_specs=[
            tok(C, 0), tok(C, 0), tok(C, 0),
            tok(D, g0), tok(D, g0 + 1), tok(D, g0 + 2),
            tok(D, 0),
            _mod_spec(nb, D, 2, lambda b, t: b),
            pl.BlockSpec((N_BRANCH, C, D), lambda b, t: (0, 0, 0)),
            pl.BlockSpec((D, D), lambda b, t: (0, 0)),
        ],
        out_specs=tok(D, 0),
        out_shape=jax.ShapeDtypeStruct((B, T, D), F32),
        compiler_params=_cparams("arbitrary", "arbitrary"),
        name="merge",
    )(attn, conf, short, P, P, P, x, mod, w_branch, w_out)


def _ffn_kernel(x_ref, g_ref, sc_ref, sh_ref, gf_ref, wg_ref, wu_ref, wo_ref, o_ref, h_scr, acc_scr):
    f = pl.program_id(2)

    @pl.when(f == 0)
    def _():
        h_scr[...] = _norm_mod(x_ref[...], g_ref[...], sc_ref[...], sh_ref[...])
        acc_scr[...] = jnp.zeros_like(acc_scr)

    h = h_scr[...]
    gate = jnp.dot(h, wg_ref[...], preferred_element_type=F32)
    up = jnp.dot(h, wu_ref[...], preferred_element_type=F32)
    act = (_silu(gate) * up).astype(BF16)
    acc_scr[...] += jnp.dot(act, wo_ref[...], preferred_element_type=F32)

    @pl.when(f == pl.num_programs(2) - 1)
    def _():
        nb, tt, d = x_ref.shape
        o_ref[...] = x_ref[...] + gf_ref[...] * acc_scr[...].reshape(nb, tt, d)


def _ffn(x, g, mod, w_in, w_out):
    B, T, D = x.shape
    nb, tt = _token_blocks(B, T)
    dff = w_out.shape[0]
    tf = dff // 2
    assert tf % LANES == 0
    nf = dff // tf
    return pl.pallas_call(
        _ffn_kernel,
        grid=(B // nb, T // tt, nf),
        in_specs=[
            pl.BlockSpec((nb, tt, D), lambda b, t, f: (b, t, 0)),
            pl.BlockSpec((1, D), lambda b, t, f: (0, 0)),
            _mod_spec(nb, D, 4, lambda b, t, f: b),
            _mod_spec(nb, D, 3, lambda b, t, f: b),
            _mod_spec(nb, D, 5, lambda b, t, f: b),
            pl.BlockSpec((D, tf), lambda b, t, f: (0, f)),
            pl.BlockSpec((D, tf), lambda b, t, f: (0, nf + f)),
            pl.BlockSpec((tf, D), lambda b, t, f: (f, 0)),
        ],
        out_specs=pl.BlockSpec((nb, tt, D), lambda b, t, f: (b, t, 0)),
        out_shape=jax.ShapeDtypeStruct((B, T, D), F32),
        scratch_shapes=[pltpu.VMEM((nb * tt, D), BF16), pltpu.VMEM((nb * tt, D), F32)],
        compiler_params=_cparams("arbitrary", "arbitrary", "arbitrary"),
        name="ffn",
    )(x, g, mod, mod, mod, w_in, w_in, w_out)


def _moe_kernel(x_ref, g_ref, sc_ref, sh_ref, gf_ref, wr_ref, br_ref, wg_ref, wu_ref, wo_ref,
                o_ref, h_scr, comb_scr, acc_scr):
    e = pl.program_id(2)
    lane = lax.broadcasted_iota(jnp.int32, comb_scr.shape, 1)

    @pl.when(e == 0)
    def _():
        h = _norm_mod(x_ref[...], g_ref[...], sc_ref[...], sh_ref[...])
        h_scr[...] = h
        acc_scr[...] = jnp.zeros_like(acc_scr)
        logits = jnp.dot(h, wr_ref[...], preferred_element_type=F32) + br_ref[...]
        logits = jnp.where(lane < N_EXPERTS, logits, -jnp.inf)
        m1 = jnp.max(logits, axis=1, keepdims=True)
        i1 = jnp.min(jnp.where(logits == m1, lane, LANES), axis=1, keepdims=True)
        rest = jnp.where(lane == i1, -jnp.inf, logits)
        m2 = jnp.max(rest, axis=1, keepdims=True)
        i2 = jnp.min(jnp.where(rest == m2, lane, LANES), axis=1, keepdims=True)
        e2 = jnp.exp(m2 - m1)
        w1 = 1.0 / (1.0 + e2)
        w2 = e2 / (1.0 + e2)
        comb_scr[...] = jnp.where(lane == i1, w1, 0.0) + jnp.where(lane == i2, w2, 0.0)

    h = h_scr[...]
    gate = jnp.dot(h, wg_ref[...], preferred_element_type=F32)
    up = jnp.dot(h, wu_ref[...], preferred_element_type=F32)
    act = (_silu(gate) * up).astype(BF16)
    y = jnp.dot(act, wo_ref[...], preferred_element_type=F32)
    c_e = jnp.sum(jnp.where(lane == e, comb_scr[...], 0.0), axis=1, keepdims=True)
    acc_scr[...] += c_e * y

    @pl.when(e == pl.num_programs(2) - 1)
    def _():
        nb, tt, d = x_ref.shape
        o_ref[...] = x_ref[...] + gf_ref[...] * acc_scr[...].reshape(nb, tt, d)


def _moe(x, g, mod, w_router, b_router, w_exp_in, w_exp_out):
    B, T, D = x.shape
    nb, tt = _token_blocks(B, T)
    ne, dfe, _ = w_exp_out.shape
    assert ne == N_EXPERTS and dfe % LANES == 0
    wr = jnp.zeros((D, LANES), BF16).at[:, :ne].set(w_router.astype(BF16))
    br = jnp.zeros((1, LANES), F32).at[0, :ne].set(b_router)
    return pl.pallas_call(
        _moe_kernel,
        grid=(B // nb, T // tt, ne),
        in_specs=[
            pl.BlockSpec((nb, tt, D), lambda b, t, e: (b, t, 0)),
            pl.BlockSpec((1, D), lambda b, t, e: (0, 0)),
            _mod_spec(nb, D, 4, lambda b, t, e: b),
            _mod_spec(nb, D, 3, lambda b, t, e: b),
            _mod_spec(nb, D, 5, lambda b, t, e: b),
            pl.BlockSpec((D, LANES), lambda b, t, e: (0, 0)),
            pl.BlockSpec((1, LANES), lambda b, t, e: (0, 0)),
            pl.BlockSpec((None, D, dfe), lambda b, t, e: (e, 0, 0)),
            pl.BlockSpec((None, D, dfe), lambda b, t, e: (e, 0, 1)),
            pl.BlockSpec((None, dfe, D), lambda b, t, e: (e, 0, 0)),
        ],
        out_specs=pl.BlockSpec((nb, tt, D), lambda b, t, e: (b, t, 0)),
        out_shape=jax.ShapeDtypeStruct((B, T, D), F32),
        scratch_shapes=[pltpu.VMEM((nb * tt, D), BF16), pltpu.VMEM((nb * tt, LANES), F32),
                        pltpu.VMEM((nb * tt, D), F32)],
        compiler_params=_cparams("arbitrary", "arbitrary", "arbitrary"),
        name="moe",
    )(x, g, mod, mod, mod, wr, br, w_exp_in, w_exp_in, w_exp_out)


def _final_kernel(x_ref, g_ref, o_ref):
    x = x_ref[...]
    ms = jnp.mean(x * x, axis=-1, keepdims=True)
    o_ref[...] = (x * lax.rsqrt(ms + EPS)) * g_ref[...]


def _final_norm(x, g):
    B, T, D = x.shape
    nb, tt = _token_blocks(B, T)
    return pl.pallas_call(
        _final_kernel,
        grid=(B // nb, T // tt),
        in_specs=[pl.BlockSpec((nb, tt, D), lambda b, t: (b, t, 0)),
                  pl.BlockSpec((1, D), lambda b, t: (0, 0))],
        out_specs=pl.BlockSpec((nb, tt, D), lambda b, t: (b, t, 0)),
        out_shape=jax.ShapeDtypeStruct((B, T, D), F32),
        compiler_params=_cparams("arbitrary", "arbitrary"),
        name="final_norm",
    )(x, g.reshape(1, D))


def _dup(a):
    return jnp.concatenate([a, a], axis=-1)


def _pad_rows(a, rows):
    return jnp.pad(a, ((0, 0), (0, rows - a.shape[1]), (0, 0)))


def _trunk(x, mods, past, p):
    B, T, D = x.shape
    depth = p["w_in"].shape[0]
    states = [[], [], [], [], []]
    for l in range(depth):
        mod = mods[l]
        P = _inproj(x, p["g_norm_mix"][l].reshape(1, D), mod, p["w_in"][l])
        k_new = P[:, :, SEC_KK:SEC_KK + HEAD_DIM]
        v_new = P[:, :, SEC_VV:SEC_VV + HEAD_DIM]
        ik_new = P[:, :, SEC_IK:SEC_IK + IDX_DIM]
        if past is None:
            attn = _attention_prompt(P, topk=min(TOPK_MAX, T // 4))
            hist_conf = jnp.zeros((B, HIST_CONF, D_CONF), F32)
            hist_short = jnp.zeros((B, HIST_SHORT, D_SC), F32)
        else:
            cache_k, cache_v, cache_ik, conf_hist, short_hist = [a[l] for a in past]
            n_keys = cache_k.shape[1] + T
            lpad = -(-n_keys // LANES) * LANES
            kk = _pad_rows(_dup(jnp.concatenate([cache_k, k_new], axis=1)), lpad)
            vv = _pad_rows(_dup(jnp.concatenate([cache_v, v_new], axis=1)), lpad)
            ik = _pad_rows(_dup(jnp.concatenate([cache_ik, ik_new], axis=1)), lpad)
            attn = _attention(P, kk, vv, ik, (0, 0, 0),
                              topk=min(TOPK_MAX, n_keys // 4), n_keys=n_keys, causal=False)
            hist_conf = jnp.pad(conf_hist, ((0, 0), (HIST_CONF - (CONF_WIDTH - 1), 0), (0, 0)))
            hist_short = jnp.pad(short_hist, ((0, 0), (HIST_SHORT - (SC_WIDTH - 1), 0), (0, 0)))
        conf, short, st_conf, st_short = _convs(
            P, hist_conf, hist_short, p["w_dw_conf"][l], p["b_dw_conf"][l], p["ln_conf_g"][l],
            p["ln_conf_b"][l], p["w_dw_short"][l])
        x = _merge(attn, conf, short, P, x, mod, p["w_branch"][l], p["w_out"][l])
        if l % 2 == 0:
            x = _ffn(x, p["g_norm_ffn"][l].reshape(1, D), mod, p["w_ffn_in"][l // 2], p["w_ffn_out"][l // 2])
        else:
            x = _moe(x, p["g_norm_ffn"][l].reshape(1, D), mod, p["w_router"][l // 2], p["b_router"][l // 2],
                     p["w_exp_in"][l // 2], p["w_exp_out"][l // 2])
        for i, s in enumerate((k_new, v_new, ik_new, st_conf[:, HIST_CONF - (CONF_WIDTH - 1):],
                               st_short[:, HIST_SHORT - (SC_WIDTH - 1):])):
            states[i].append(s)
    y = _final_norm(x, p["g_final"])
    return y, tuple(jnp.stack(s) for s in states)


def kernel(x_prompt, x_sample, cache_k, cache_v, cache_idx_k, state_conv_conformer, state_conv_short,
           c_prompt, c_sample, w_ada, b_ada, g_norm_mix, g_norm_ffn, w_in, w_dw_conf, b_dw_conf,
           ln_conf_g, ln_conf_b, w_dw_short, w_branch, w_out, w_ffn_in, w_ffn_out, w_router, b_router,
           w_exp_in, w_exp_out, g_final):
    depth, D, _ = w_in.shape
    bp = x_prompt.shape[0]
    mods = _ada(jnp.concatenate([c_prompt, c_sample], axis=0), w_ada, b_ada)
    mods = mods.reshape(depth, -1, 6, 1, D)
    params = {
        "g_norm_mix": g_norm_mix, "g_norm_ffn": g_norm_ffn,
        "w_in": jnp.stack([_rearrange_w_in(w_in[l]) for l in range(depth)]),
        "w_dw_conf": w_dw_conf, "b_dw_conf": b_dw_conf, "ln_conf_g": ln_conf_g, "ln_conf_b": ln_conf_b,
        "w_dw_short": w_dw_short, "w_branch": w_branch.astype(BF16), "w_out": w_out.astype(BF16),
        "w_ffn_in": w_ffn_in.astype(BF16), "w_ffn_out": w_ffn_out.astype(BF16),
        "w_router": w_router, "b_router": b_router,
        "w_exp_in": w_exp_in.astype(BF16), "w_exp_out": w_exp_out.astype(BF16), "g_final": g_final,
    }
    y_p, (pk, pv, pik, pconf, pshort) = _trunk(x_prompt, mods[:, :bp], None, params)
    y_s, (sk, sv, sik, sconf, sshort) = _trunk(
        x_sample, mods[:, bp:],
        (cache_k, cache_v, cache_idx_k, state_conv_conformer, state_conv_short), params)
    return (y_p, y_s, pk, pv, pik, pconf, pshort, sk, sv, sik, sconf, sshort)
```

```python
import functools

import jax
import jax.numpy as jnp
import numpy as np
from jax import lax
from jax.experimental import pallas as pl
from jax.experimental.pallas import tpu as pltpu

CHUNK = 64
N_HEADS = 8
HEAD_DIM = 64
IDX_HEADS = 8
IDX_DIM = 64
TOPK_MAX = 256
D_CONF = 512
CONF_WIDTH = 31
D_SC = 512
SC_WIDTH = 3
N_BRANCH = 3
N_EXPERTS = 8
EPS = 1e-6

LANES = 128
SUBLANES = 8
VMEM_LIMIT = 56 * 1024 * 1024
INT_MIN = np.int32(-2 ** 31)

F32 = jnp.float32
BF16 = jnp.bfloat16

SEC_Q = 0
SEC_IQ = 512
SEC_KK = 1024
SEC_VV = 1152
SEC_IK = 1280
SEC_IW = 1408
SEC_GLU_A = 1536
SEC_GLU_G = 2048
SEC_SC_B = 2560
SEC_SC_C = 3072
SEC_SC_X = 3584
SEC_GATES = 4096
P_WIDTH = 7168
HIST_CONF = 32
HIST_SHORT = 8


def _cparams(*sem):
    return pltpu.CompilerParams(dimension_semantics=sem, vmem_limit_bytes=VMEM_LIMIT)


def _sigmoid(x):
    return 0.5 * jnp.tanh(0.5 * x) + 0.5


def _silu(x):
    return x * _sigmoid(x)


def _token_blocks(B, T, rows=1024):
    if T >= rows:
        assert T % rows == 0
        return 1, rows
    nb = min(B, rows // T)
    assert B % nb == 0
    return nb, T


def _ada_kernel(c_ref, w_ref, b_ref, o_ref):
    c = _silu(c_ref[...]).astype(BF16)
    o_ref[...] = jnp.dot(c, w_ref[...].astype(BF16), preferred_element_type=F32) + b_ref[...]


def _ada(c, w_ada, b_ada):
    depth, d, n = w_ada.shape
    bc = c.shape[0]
    tn = 1536
    assert n % tn == 0
    return pl.pallas_call(
        _ada_kernel,
        grid=(depth, n // tn),
        in_specs=[
            pl.BlockSpec((bc, d), lambda l, j: (0, 0)),
            pl.BlockSpec((None, d, tn), lambda l, j: (l, 0, j)),
            pl.BlockSpec((None, 1, tn), lambda l, j: (l, 0, j)),
        ],
        out_specs=pl.BlockSpec((None, bc, tn), lambda l, j: (l, 0, j)),
        out_shape=jax.ShapeDtypeStruct((depth, bc, n), F32),
        compiler_params=_cparams("arbitrary", "arbitrary"),
        name="ada",
    )(c, w_ada, b_ada.reshape(depth, 1, n))


def _norm_mod(x, g, scale, shift):
    ms = jnp.mean(x * x, axis=-1, keepdims=True)
    xn = (x * lax.rsqrt(ms + EPS)) * g
    h = xn * (1.0 + scale) + shift
    nb, tt, d = x.shape
    return h.reshape(nb * tt, d).astype(BF16)


def _mod_spec(nb, d, which, bmap):
    return pl.BlockSpec((nb, None, 1, d), lambda *a: (bmap(*a), which, 0, 0))


def _inproj_kernel(x_ref, g_ref, sc_ref, sh_ref, w_ref, o_ref, k_ref, v_ref, ik_ref, h_scr):
    j = pl.program_id(2)

    @pl.when(j == 0)
    def _():
        h_scr[...] = _norm_mod(x_ref[...], g_ref[...], sc_ref[...], sh_ref[...])

    nb, tt, tn = o_ref.shape
    res = jnp.dot(h_scr[...], w_ref[...], preferred_element_type=F32)
    o_ref[...] = res.reshape(nb, tt, tn)

    assert SEC_KK // tn == SEC_IK // tn

    @pl.when(j == SEC_KK // tn)
    def _():
        for ref, sec in ((k_ref, SEC_KK), (v_ref, SEC_VV), (ik_ref, SEC_IK)):
            c0 = sec % tn
            ref[...] = res[:, c0:c0 + ref.shape[-1]].reshape(ref.shape)


def _inproj(x, g, mod, w):
    B, T, D = x.shape
    nb, tt = _token_blocks(B, T)
    tn = 1024
    width = w.shape[1]
    assert width % tn == 0
    grid = (B // nb, T // tt, width // tn)

    def row_out(width_):
        return pl.BlockSpec((nb, tt, width_), lambda b, t, j: (b, t, 0))

    return pl.pallas_call(
        _inproj_kernel,
        grid=grid,
        in_specs=[
            pl.BlockSpec((nb, tt, D), lambda b, t, j: (b, t, 0)),
            pl.BlockSpec((1, D), lambda b, t, j: (0, 0)),
            _mod_spec(nb, D, 1, lambda b, t, j: b),
            _mod_spec(nb, D, 0, lambda b, t, j: b),
            pl.BlockSpec((D, tn), lambda b, t, j: (0, j)),
        ],
        out_specs=[pl.BlockSpec((nb, tt, tn), lambda b, t, j: (b, t, j)),
                   row_out(HEAD_DIM), row_out(HEAD_DIM), row_out(IDX_DIM)],
        out_shape=[jax.ShapeDtypeStruct((B, T, width), F32),
                   jax.ShapeDtypeStruct((B, T, HEAD_DIM), F32),
                   jax.ShapeDtypeStruct((B, T, HEAD_DIM), F32),
                   jax.ShapeDtypeStruct((B, T, IDX_DIM), F32)],
        scratch_shapes=[pltpu.VMEM((nb * tt, D), BF16)],
        compiler_params=_cparams("arbitrary", "arbitrary", "arbitrary"),
        name="inproj",
    )(x, g, mod, mod, w)


def _rearrange_w_in(w_in):
    d = w_in.shape[0]
    sizes = (N_HEADS * HEAD_DIM, HEAD_DIM, HEAD_DIM, IDX_HEADS * IDX_DIM, IDX_DIM, IDX_HEADS,
             2 * D_CONF, 3 * D_SC, N_BRANCH * d)
    offs = np.cumsum((0,) + sizes)
    q, k, v, iq, ik, iw, glu, sc, gates = [w_in[:, offs[i]:offs[i + 1]] for i in range(9)]
    iw_pad = jnp.zeros((d, LANES - IDX_HEADS), w_in.dtype)
    out = jnp.concatenate([q, iq, k, k, v, v, ik, ik, iw, iw_pad, glu, sc, gates], axis=1)
    assert out.shape[1] == P_WIDTH
    return out.astype(BF16)


def _stack_heads(x):
    rows = x.shape[0]
    lane = lax.broadcasted_iota(jnp.int32, (rows, LANES), 1)
    lo = lane < HEAD_DIM
    parts = []
    for h in range(N_HEADS):
        slab = x[:, (h // 2) * LANES:(h // 2 + 1) * LANES]
        parts.append(jnp.where(lo if h % 2 == 0 else jnp.logical_not(lo), slab, 0.0))
    return jnp.concatenate(parts, axis=0).astype(BF16)


def _count(mask):
    return jnp.sum(jnp.where(mask, 1.0, 0.0), axis=1, keepdims=True)


def _attn_kernel(q_ref, iq_ref, iw_ref, kk_ref, vv_ref, ik_ref, o_ref, key_scr, sel_scr, p_scr,
                 *, topk, n_keys, causal):
    R = CHUNK
    L = kk_ref.shape[0]
    nt = (((1,), (1,)), ((), ()))
    if causal:
        limit = (pl.program_id(1) + 1) * CHUNK
    else:
        limit = n_keys
    pos = lax.broadcasted_iota(jnp.int32, (R, L), 1)
    admissible = pos < limit

    rel = lax.dot_general(_stack_heads(iq_ref[...]), ik_ref[...].astype(BF16), nt,
                          preferred_element_type=F32)
    iw = iw_ref[...]
    score = jnp.zeros((R, L), F32)
    for h in range(IDX_HEADS):
        r = jnp.maximum(rel[h * R:(h + 1) * R] * (IDX_DIM ** -0.5), 0.0)
        score = score + (iw[:, h:h + 1] * (IDX_HEADS ** -0.5)) * r

    bits = lax.bitcast_convert_type(score, jnp.int32)
    key = jnp.where(bits < 0, bits ^ jnp.int32(0x7FFFFFFF), bits)
    key_scr[...] = jnp.where(admissible, key, INT_MIN)

    def search(i, thr):
        cand = thr + lax.shift_left(jnp.int32(1), 31 - i)
        cnt = _count(key_scr[...] >= cand)
        return jnp.where(cnt >= topk, cand, thr)

    thr = lax.fori_loop(0, 32, search, jnp.full((R, 1), INT_MIN, jnp.int32))
    keys = key_scr[...]
    n_ge = _count(keys >= thr)
    has_tie = jnp.max(n_ge) > topk

    @pl.when(jnp.logical_not(has_tie))
    def _():
        sel_scr[...] = jnp.where(jnp.logical_and(key_scr[...] >= thr, admissible), 1.0, 0.0)

    @pl.when(has_tie)
    def _():
        kk_ = key_scr[...]
        eq = kk_ == thr
        need = topk - _count(kk_ > thr)
        nbits = int(L - 1).bit_length()

        def tie_search(i, d):
            cand = d + lax.shift_left(jnp.int32(1), nbits - 1 - i)
            f = _count(jnp.logical_and(eq, pos < cand))
            return jnp.where(f < need, cand, d)

        d = lax.fori_loop(0, nbits, tie_search, jnp.zeros((R, 1), jnp.int32))
        chosen = jnp.logical_or(kk_ > thr, jnp.logical_and(eq, pos <= d))
        sel_scr[...] = jnp.where(jnp.logical_and(chosen, admissible), 1.0, 0.0)

    logits = lax.dot_general(_stack_heads(q_ref[...]), kk_ref[...].astype(BF16), nt,
                             preferred_element_type=F32) * (HEAD_DIM ** -0.5)
    sel = sel_scr[...] > 0.5
    denoms = []
    for h in range(N_HEADS):
        lg = jnp.where(sel, logits[h * R:(h + 1) * R], -jnp.inf)
        m = jnp.max(lg, axis=1, keepdims=True)
        p = jnp.exp(lg - m)
        denoms.append(jnp.sum(p, axis=1, keepdims=True))
        p_scr[h * R:(h + 1) * R, :] = p.astype(BF16)
    o = jnp.dot(p_scr[...], vv_ref[...].astype(BF16), preferred_element_type=F32)
    lo = lax.broadcasted_iota(jnp.int32, (R, LANES), 1) < HEAD_DIM
    for hp in range(N_HEADS // 2):
        even = o[(2 * hp) * R:(2 * hp + 1) * R] / denoms[2 * hp]
        odd = o[(2 * hp + 1) * R:(2 * hp + 2) * R] / denoms[2 * hp + 1]
        o_ref[:, hp * LANES:(hp + 1) * LANES] = jnp.where(lo, even, odd)


def _attention(P, kk, vv, ik, kv_blocks, *, topk, n_keys, causal):
    B, T, _ = P.shape
    L = kk.shape[1]
    nq = T // CHUNK
    dq = N_HEADS * HEAD_DIM
    kern = functools.partial(_attn_kernel, topk=topk, n_keys=n_keys, causal=causal)
    ck, cv, ci = kv_blocks
    return pl.pallas_call(
        kern,
        grid=(B, nq),
        in_specs=[
            pl.BlockSpec((None, CHUNK, dq), lambda b, j: (b, j, SEC_Q // dq)),
            pl.BlockSpec((None, CHUNK, dq), lambda b, j: (b, j, SEC_IQ // dq)),
            pl.BlockSpec((None, CHUNK, LANES), lambda b, j: (b, j, SEC_IW // LANES)),
            pl.BlockSpec((None, L, LANES), lambda b, j: (b, 0, ck)),
            pl.BlockSpec((None, L, LANES), lambda b, j: (b, 0, cv)),
            pl.BlockSpec((None, L, LANES), lambda b, j: (b, 0, ci)),
        ],
        out_specs=pl.BlockSpec((None, CHUNK, dq), lambda b, j: (b, j, 0)),
        out_shape=jax.ShapeDtypeStruct((B, T, dq), F32),
        scratch_shapes=[
            pltpu.VMEM((CHUNK, L), jnp.int32),
            pltpu.VMEM((CHUNK, L), F32),
            pltpu.VMEM((N_HEADS * CHUNK, L), BF16),
        ],
        compiler_params=_cparams("arbitrary", "arbitrary"),
        name="attn",
    )(P, P, P, kk, vv, ik)


QT = 2 * CHUNK
KB = 256


SUM_ROWS = 64


def _colcount(n_rows, pred):
    acc = jnp.zeros((SUM_ROWS, LANES), F32)
    for k in range(n_rows // SUM_ROWS):
        acc = acc + jnp.where(pred(slice(k * SUM_ROWS, (k + 1) * SUM_ROWS)), 1.0, 0.0)
    return jnp.sum(acc, axis=0, keepdims=True)


def _attn_t_kernel(q_ref, iq_ref, iw_ref, kk_ref, vv_ref, ik_ref, o_ref,
                   key_scr, sel_scr, lg_scr, p_scr, vve_scr, *, topk, tile0):
    L = kk_ref.shape[0]
    nkb = L // KB
    nh = N_HEADS
    nt = (((1,), (1,)), ((), ()))
    tn = (((0,), (0,)), ((), ()))
    tile = tile0 + pl.program_id(1)
    lane = lax.broadcasted_iota(jnp.int32, (1, LANES), 1)
    limit = tile * QT + jnp.where(lane < CHUNK, CHUNK, QT)

    w_t = jnp.transpose(iw_ref[...])
    w_rows = [(w_t[h:h + 1, :] * (IDX_HEADS ** -0.5)) * (IDX_DIM ** -0.5) for h in range(nh)]
    qs_i = _stack_heads(iq_ref[...])
    for kb in range(nkb):
        rows = slice(kb * KB, (kb + 1) * KB)
        rel = lax.dot_general(ik_ref[rows, :].astype(BF16), qs_i, nt, preferred_element_type=F32)
        score = jnp.zeros((KB, LANES), F32)
        for h in range(nh):
            score = score + w_rows[h] * jnp.maximum(rel[:, h * QT:(h + 1) * QT], 0.0)
        bits = lax.bitcast_convert_type(score, jnp.int32)
        key = jnp.where(bits < 0, bits ^ jnp.int32(0x7FFFFFFF), bits)
        pos = kb * KB + lax.broadcasted_iota(jnp.int32, (KB, LANES), 0)
        key_scr[rows, :] = jnp.where(pos < limit, key, INT_MIN)
        vve_scr[rows, 0:LANES] = vv_ref[rows, :].astype(BF16)
        vve_scr[rows, LANES:2 * LANES] = jnp.ones((KB, LANES), BF16)

    def search(i, thr):
        cand = thr + lax.shift_left(jnp.int32(1), 31 - i)
        cnt = _colcount(L, lambda r: key_scr[r, :] >= cand)
        return jnp.where(cnt >= topk, cand, thr)

    thr = jnp.full((1, LANES), INT_MIN, jnp.int32)
    if L > topk:
        thr = lax.fori_loop(0, 32, search, thr)
    has_tie = jnp.max(_colcount(L, lambda r: key_scr[r, :] >= thr)) > topk

    def row_pos(r):
        return r.start + lax.broadcasted_iota(jnp.int32, (r.stop - r.start, LANES), 0)

    @pl.when(jnp.logical_not(has_tie))
    def _():
        for kb in range(nkb):
            r = slice(kb * KB, (kb + 1) * KB)
            chosen = jnp.logical_and(key_scr[r, :] >= thr, row_pos(r) < limit)
            sel_scr[r, :] = jnp.where(chosen, 1.0, 0.0)

    @pl.when(has_tie)
    def _():
        need = topk - _colcount(L, lambda r: key_scr[r, :] > thr)
        nbits = int(L - 1).bit_length()

        def tie_search(i, d):
            cand = d + lax.shift_left(jnp.int32(1), nbits - 1 - i)
            f = _colcount(L, lambda r: jnp.logical_and(key_scr[r, :] == thr, row_pos(r) < cand))
            return jnp.where(f < need, cand, d)

        d = lax.fori_loop(0, nbits, tie_search, jnp.zeros((1, LANES), jnp.int32))
        for kb in range(nkb):
            r = slice(kb * KB, (kb + 1) * KB)
            keys = key_scr[r, :]
            pos = row_pos(r)
            chosen = jnp.logical_or(keys > thr, jnp.logical_and(keys == thr, pos <= d))
            sel_scr[r, :] = jnp.where(jnp.logical_and(chosen, pos < limit), 1.0, 0.0)

    qs = _stack_heads(q_ref[...] * (HEAD_DIM ** -0.5))
    m_run = [jnp.full((1, QT), -jnp.inf, F32) for _ in range(nh)]
    for kb in range(nkb):
        rows = slice(kb * KB, (kb + 1) * KB)
        lg = lax.dot_general(kk_ref[rows, :].astype(BF16), qs, nt, preferred_element_type=F32)
        sel = sel_scr[rows, :] > 0.5
        for h in range(nh):
            lgm = jnp.where(sel, lg[:, h * QT:(h + 1) * QT], -jnp.inf)
            lg_scr[rows, h * QT:(h + 1) * QT] = lgm
            m_run[h] = jnp.maximum(m_run[h], jnp.max(lgm, axis=0, keepdims=True))
    m_all = jnp.concatenate(m_run, axis=1)

    acc = jnp.zeros((nh * QT, 2 * LANES), F32)
    for kb in range(nkb):
        rows = slice(kb * KB, (kb + 1) * KB)
        p_scr[rows, :] = jnp.exp(lg_scr[rows, :] - m_all).astype(BF16)
        acc = acc + lax.dot_general(p_scr[rows, :], vve_scr[rows, :], tn, preferred_element_type=F32)
    on = acc[:, 0:LANES] / acc[:, LANES:2 * LANES]
    lo = lax.broadcasted_iota(jnp.int32, (QT, LANES), 1) < HEAD_DIM
    for hp in range(nh // 2):
        even = on[(2 * hp) * QT:(2 * hp + 1) * QT]
        odd = on[(2 * hp + 1) * QT:(2 * hp + 2) * QT]
        o_ref[:, hp * LANES:(hp + 1) * LANES] = jnp.where(lo, even, odd)


def _attention_prompt(P, *, topk):
    B, T, _ = P.shape
    dq = N_HEADS * HEAD_DIM
    assert T % KB == 0 and QT == LANES
    tiles_per_class = KB // QT
    ck, cv, ci = SEC_KK // LANES, SEC_VV // LANES, SEC_IK // LANES
    outs = []
    for c in range(T // KB):
        L = KB * (c + 1)
        t0 = c * tiles_per_class

        def qspec(width, col, t0=t0):
            return pl.BlockSpec((None, QT, width), lambda b, i: (b, t0 + i, col))

        def kspec(col, L=L):
            return pl.BlockSpec((None, L, LANES), lambda b, i: (b, 0, col))

        outs.append(pl.pallas_call(
            functools.partial(_attn_t_kernel, topk=topk, tile0=t0),
            grid=(B, tiles_per_class),
            in_specs=[qspec(dq, SEC_Q // dq), qspec(dq, SEC_IQ // dq), qspec(LANES, SEC_IW // LANES),
                      kspec(ck), kspec(cv), kspec(ci)],
            out_specs=pl.BlockSpec((None, QT, dq), lambda b, i: (b, i, 0)),
            out_shape=jax.ShapeDtypeStruct((B, tiles_per_class * QT, dq), F32),
            scratch_shapes=[
                pltpu.VMEM((L, LANES), jnp.int32),
                pltpu.VMEM((L, LANES), F32),
                pltpu.VMEM((L, N_HEADS * QT), F32),
                pltpu.VMEM((L, N_HEADS * QT), BF16),
                pltpu.VMEM((L, 2 * LANES), BF16),
            ],
            compiler_params=_cparams("arbitrary", "arbitrary"),
            name=f"attn_prompt_L{L}",
        )(P, P, P, P, P, P))
    return jnp.concatenate(outs, axis=1)


def _conv_kernel(a_ref, g_ref, pa_ref, pg_ref, hc_ref, sb_ref, sc_ref, sx_ref, psc_ref, psx_ref,
                 hs_ref, wc_ref, bc_ref, lg_ref, lb_ref, ws_ref,
                 conf_ref, short_ref, stc_ref, sts_ref, u_scr, us_scr, sh_scr, *, rows):
    t = pl.program_id(1)
    TT = a_ref.shape[0]

    @pl.when(t == 0)
    def _():
        u_scr[0:HIST_CONF, :] = hc_ref[...]
        us_scr[0:HIST_SHORT, :] = hs_ref[...]

    @pl.when(t > 0)
    def _():
        u_scr[0:HIST_CONF, :] = pa_ref[...] * _sigmoid(pg_ref[...])
        us_scr[0:HIST_SHORT, :] = psc_ref[...] * psx_ref[...]

    u_scr[HIST_CONF:, :] = a_ref[...] * _sigmoid(g_ref[...])
    us_scr[HIST_SHORT:, :] = sc_ref[...] * sx_ref[...]

    stc_ref[...] = u_scr[TT:TT + HIST_CONF, :]
    sts_ref[...] = us_scr[TT:TT + HIST_SHORT, :]

    off_c = HIST_CONF - (CONF_WIDTH - 1)
    off_s = HIST_SHORT - (SC_WIDTH - 1)

    s = jnp.zeros((TT, D_SC), F32)
    for w in range(SC_WIDTH):
        s = s + us_scr[off_s + w:off_s + w + TT, :] * ws_ref[w:w + 1, :]
    short_ref[...] = (sb_ref[...] * s).astype(BF16)

    for ph in range(SUBLANES):
        span = TT + HIST_CONF - (SUBLANES if ph else 0)
        sh_scr[ph, 0:span, :] = u_scr[ph:ph + span, :]

    for c in range(D_CONF // LANES):
        cs = slice(c * LANES, (c + 1) * LANES)

        def body(r, carry, cs=cs):
            r0 = pl.multiple_of(r * rows, rows)
            acc = jnp.zeros((rows, LANES), F32) + bc_ref[:, cs]
            for w in range(CONF_WIDTH):
                o = off_c + w
                tap = sh_scr[o % SUBLANES, pl.ds(r0 + (o // SUBLANES) * SUBLANES, rows), cs]
                acc = acc + tap * wc_ref[w:w + 1, cs]
            u_scr[pl.ds(r0, rows), cs] = acc
            return carry

        lax.fori_loop(0, TT // rows, body, 0)

    conv = u_scr[0:TT, :]
    mu = jnp.mean(conv, axis=-1, keepdims=True)
    xc = conv - mu
    var = jnp.mean(xc * xc, axis=-1, keepdims=True)
    y = (xc * lax.rsqrt(var + EPS)) * lg_ref[...] + lb_ref[...]
    conf_ref[...] = _silu(y).astype(BF16)


def _convs(P, hist_conf, hist_short, w_dw_conf, b_dw_conf, ln_g, ln_b, w_dw_short):
    B, T, _ = P.shape
    TT = min(T, 512)
    nt = T // TT
    rows = 64
    C = D_CONF

    def sec(off):
        return pl.BlockSpec((None, TT, C), lambda b, t: (b, t, off // C))

    def prev(off, h):
        return pl.BlockSpec((None, h, C), lambda b, t: (b, jnp.maximum(t * (TT // h) - 1, 0), off // C))

    def hist(h):
        return pl.BlockSpec((None, h, C), lambda b, t: (b, 0, 0))

    def vec(r):
        return pl.BlockSpec((r, C), lambda b, t: (0, 0))

    wc = jnp.zeros((32, C), F32).at[:CONF_WIDTH].set(w_dw_conf)
    ws = jnp.zeros((8, C), F32).at[:SC_WIDTH].set(w_dw_short)
    outs = pl.pallas_call(
        functools.partial(_conv_kernel, rows=rows),
        grid=(B, nt),
        in_specs=[
            sec(SEC_GLU_A), sec(SEC_GLU_G), prev(SEC_GLU_A, HIST_CONF), prev(SEC_GLU_G, HIST_CONF),
            hist(HIST_CONF),
            sec(SEC_SC_B), sec(SEC_SC_C), sec(SEC_SC_X), prev(SEC_SC_C, HIST_SHORT),
            prev(SEC_SC_X, HIST_SHORT), hist(HIST_SHORT),
            vec(32), vec(1), vec(1), vec(1), vec(8),
        ],
        out_specs=[
            pl.BlockSpec((None, TT, C), lambda b, t: (b, t, 0)),
            pl.BlockSpec((None, TT, C), lambda b, t: (b, t, 0)),
            pl.BlockSpec((None, HIST_CONF, C), lambda b, t: (b, 0, 0)),
            pl.BlockSpec((None, HIST_SHORT, C), lambda b, t: (b, 0, 0)),
        ],
        out_shape=[
            jax.ShapeDtypeStruct((B, T, C), BF16),
            jax.ShapeDtypeStruct((B, T, C), BF16),
            jax.ShapeDtypeStruct((B, HIST_CONF, C), F32),
            jax.ShapeDtypeStruct((B, HIST_SHORT, C), F32),
        ],
        scratch_shapes=[
            pltpu.VMEM((HIST_CONF + TT, C), F32),
            pltpu.VMEM((HIST_SHORT + TT, C), F32),
            pltpu.VMEM((SUBLANES, HIST_CONF + TT, C), F32),
        ],
        compiler_params=_cparams("arbitrary", "arbitrary"),
        name="convs",
    )(P, P, P, P, hist_conf, P, P, P, P, P, hist_short,
      wc, b_dw_conf.reshape(1, C), ln_g.reshape(1, C), ln_b.reshape(1, C), ws)
    return outs


def _merge_kernel(attn_ref, conf_ref, short_ref, ga_ref, gb_ref, gc_ref, x_ref, gm_ref,
                  wb_ref, wo_ref, o_ref):
    nb, tt, d = x_ref.shape
    n = nb * tt

    def branch(ref, i):
        v = ref[...].reshape(n, ref.shape[-1]).astype(BF16)
        return jnp.dot(v, wb_ref[i], preferred_element_type=F32)

    merged = (_sigmoid(ga_ref[...].reshape(n, d)) * branch(attn_ref, 0)
              + _sigmoid(gb_ref[...].reshape(n, d)) * branch(conf_ref, 1)
              + _sigmoid(gc_ref[...].reshape(n, d)) * branch(short_ref, 2))
    out = jnp.dot(merged.astype(BF16), wo_ref[...], preferred_element_type=F32)
    o_ref[...] = x_ref[...] + gm_ref[...] * out.reshape(nb, tt, d)


def _merge(attn, conf, short, P, x, mod, w_branch, w_out):
    B, T, D = x.shape
    nb, tt = _token_blocks(B, T, 512)
    C = attn.shape[-1]

    def tok(width, col):
        return pl.BlockSpec((nb, tt, width), lambda b, t: (b, t, col))

    g0 = SEC_GATES // D
    return pl.pallas_call(
        _merge_kernel,
        grid=(B // nb, T // tt),
        in_specs=[
            tok(C, 0), tok(C, 0), tok(C, 0),
            tok(D, g0), tok(D, g0 + 1), tok(D, g0 + 2),
            tok(D, 0),
            _mod_spec(nb, D, 2, lambda b, t: b),
            pl.BlockSpec((N_BRANCH, C, D), lambda b, t: (0, 0, 0)),
            pl.BlockSpec((D, D), lambda b, t: (0, 0)),
        ],
        out_specs=tok(D, 0),
        out_shape=jax.ShapeDtypeStruct((B, T, D), F32),
        compiler_params=_cparams("arbitrary", "arbitrary"),
        name="merge",
    )(attn, conf, short, P, P, P, x, mod, w_branch, w_out)


def _ffn_kernel(x_ref, g_ref, sc_ref, sh_ref, gf_ref, wg_ref, wu_ref, wo_ref, o_ref, h_scr, acc_scr):
    f = pl.program_id(2)

    @pl.when(f == 0)
    def _():
        h_scr[...] = _norm_mod(x_ref[...], g_ref[...], sc_ref[...], sh_ref[...])
        acc_scr[...] = jnp.zeros_like(acc_scr)

    h = h_scr[...]
    gate = jnp.dot(h, wg_ref[...], preferred_element_type=F32)
    up = jnp.dot(h, wu_ref[...], preferred_element_type=F32)
    act = (_silu(gate) * up).astype(BF16)
    acc_scr[...] += jnp.dot(act, wo_ref[...], preferred_element_type=F32)

    @pl.when(f == pl.num_programs(2) - 1)
    def _():
        nb, tt, d = x_ref.shape
        o_ref[...] = x_ref[...] + gf_ref[...] * acc_scr[...].reshape(nb, tt, d)


def _ffn(x, g, mod, w_in, w_out):
    B, T, D = x.shape
    nb, tt = _token_blocks(B, T)
    dff = w_out.shape[0]
    tf = dff // 2
    assert tf % LANES == 0
    nf = dff // tf
    return pl.pallas_call(
        _ffn_kernel,
        grid=(B // nb, T // tt, nf),
        in_specs=[
            pl.BlockSpec((nb, tt, D), lambda b, t, f: (b, t, 0)),
            pl.BlockSpec((1, D), lambda b, t, f: (0, 0)),
            _mod_spec(nb, D, 4, lambda b, t, f: b),
            _mod_spec(nb, D, 3, lambda b, t, f: b),
            _mod_spec(nb, D, 5, lambda b, t, f: b),
            pl.BlockSpec((D, tf), lambda b, t, f: (0, f)),
            pl.BlockSpec((D, tf), lambda b, t, f: (0, nf + f)),
            pl.BlockSpec((tf, D), lambda b, t, f: (f, 0)),
        ],
        out_specs=pl.BlockSpec((nb, tt, D), lambda b, t, f: (b, t, 0)),
        out_shape=jax.ShapeDtypeStruct((B, T, D), F32),
        scratch_shapes=[pltpu.VMEM((nb * tt, D), BF16), pltpu.VMEM((nb * tt, D), F32)],
        compiler_params=_cparams("arbitrary", "arbitrary", "arbitrary"),
        name="ffn",
    )(x, g, mod, mod, mod, w_in, w_in, w_out)


MOE_CHUNK = 128


def _moe_kernel(x_ref, g_ref, sc_ref, sh_ref, gf_ref, wr_ref, br_ref, wg_ref, wu_ref, wo_ref,
                o_ref, h_scr, comb_scr, rank_scr, rank_t_scr, cnt_scr, acc_scr):
    e = pl.program_id(2)
    tm = h_scr.shape[0]
    lane = lax.broadcasted_iota(jnp.int32, (tm, LANES), 1)

    @pl.when(e == 0)
    def _():
        h = _norm_mod(x_ref[...], g_ref[...], sc_ref[...], sh_ref[...])
        h_scr[...] = h
        acc_scr[...] = jnp.zeros_like(acc_scr)
        logits = jnp.dot(h, wr_ref[...], preferred_element_type=F32) + br_ref[...]
        logits = jnp.where(lane < N_EXPERTS, logits, -jnp.inf)
        m1 = jnp.max(logits, axis=1, keepdims=True)
        i1 = jnp.min(jnp.where(logits == m1, lane, LANES), axis=1, keepdims=True)
        rest = jnp.where(lane == i1, -jnp.inf, logits)
        m2 = jnp.max(rest, axis=1, keepdims=True)
        i2 = jnp.min(jnp.where(rest == m2, lane, LANES), axis=1, keepdims=True)
        e2 = jnp.exp(m2 - m1)
        w1 = 1.0 / (1.0 + e2)
        w2 = e2 / (1.0 + e2)
        comb = jnp.where(lane == i1, w1, 0.0) + jnp.where(lane == i2, w2, 0.0)
        comb_scr[...] = comb
        routed = comb > 0.0
        before = (lax.broadcasted_iota(jnp.int32, (tm, tm), 1)
                  < lax.broadcasted_iota(jnp.int32, (tm, tm), 0))
        rank = jnp.dot(jnp.where(before, 1.0, 0.0).astype(BF16),
                       jnp.where(routed, 1.0, 0.0).astype(BF16), preferred_element_type=F32)
        rank = jnp.where(routed, rank, -1.0)
        rank_scr[...] = rank
        rank_t = jnp.transpose(rank)
        for ex in range(N_EXPERTS):
            rank_t_scr[ex] = jnp.broadcast_to(rank_t[ex:ex + 1, :], (SUBLANES, tm))
        cnt_scr[...] = jnp.broadcast_to(
            jnp.sum(jnp.where(routed, 1.0, 0.0), axis=0, keepdims=True), (SUBLANES, LANES))

    lane8 = lax.broadcasted_iota(jnp.int32, (SUBLANES, LANES), 1)
    n_rows = jnp.max(jnp.where(lane8 == e, cnt_scr[...], 0.0)).astype(jnp.int32)
    n_chunks = (n_rows + (MOE_CHUNK - 1)) // MOE_CHUNK
    mine = lane == e
    rank_col = jnp.sum(jnp.where(mine, rank_scr[...], 0.0), axis=1, keepdims=True)
    comb_col = jnp.sum(jnp.where(mine, comb_scr[...], 0.0), axis=1, keepdims=True)
    rank_row = rank_t_scr[e][0:1, :]
    slot_sub = lax.broadcasted_iota(jnp.int32, (MOE_CHUNK, tm), 0).astype(F32)
    slot_lane = lax.broadcasted_iota(jnp.int32, (tm, 2 * MOE_CHUNK), 1)
    slot_lane = jnp.where(slot_lane >= MOE_CHUNK, slot_lane - MOE_CHUNK, slot_lane).astype(F32)

    def chunk(c, carry):
        base = (c * MOE_CHUNK).astype(F32)
        gather = jnp.where(rank_row == slot_sub + base, 1.0, 0.0).astype(BF16)
        hs = jnp.dot(gather, h_scr[...], preferred_element_type=F32).astype(BF16)
        gate = jnp.dot(hs, wg_ref[...], preferred_element_type=F32)
        up = jnp.dot(hs, wu_ref[...], preferred_element_type=F32)
        act = (_silu(gate) * up).astype(BF16)
        y = jnp.dot(act, wo_ref[...], preferred_element_type=F32)
        y_hi = y.astype(BF16)
        y_lo = (y - y_hi.astype(F32)).astype(BF16)
        scatter = jnp.where(rank_col == slot_lane + base, 1.0, 0.0).astype(BF16)
        back = jnp.dot(scatter, jnp.concatenate([y_hi, y_lo], axis=0), preferred_element_type=F32)
        acc_scr[...] += comb_col * back
        return carry

    lax.fori_loop(0, n_chunks, chunk, 0)

    @pl.when(e == pl.num_programs(2) - 1)
    def _():
        nb, tt, d = x_ref.shape
        o_ref[...] = x_ref[...] + gf_ref[...] * acc_scr[...].reshape(nb, tt, d)


def _moe(x, g, mod, w_router, b_router, w_exp_in, w_exp_out):
    B, T, D = x.shape
    nb, tt = _token_blocks(B, T)
    ne, dfe, _ = w_exp_out.shape
    assert ne == N_EXPERTS and dfe % LANES == 0
    wr = jnp.zeros((D, LANES), BF16).at[:, :ne].set(w_router.astype(BF16))
    br = jnp.zeros((1, LANES), F32).at[0, :ne].set(b_router)
    return pl.pallas_call(
        _moe_kernel,
        grid=(B // nb, T // tt, ne),
        in_specs=[
            pl.BlockSpec((nb, tt, D), lambda b, t, e: (b, t, 0)),
            pl.BlockSpec((1, D), lambda b, t, e: (0, 0)),
            _mod_spec(nb, D, 4, lambda b, t, e: b),
            _mod_spec(nb, D, 3, lambda b, t, e: b),
            _mod_spec(nb, D, 5, lambda b, t, e: b),
            pl.BlockSpec((D, LANES), lambda b, t, e: (0, 0)),
            pl.BlockSpec((1, LANES), lambda b, t, e: (0, 0)),
            pl.BlockSpec((None, D, dfe), lambda b, t, e: (e, 0, 0)),
            pl.BlockSpec((None, D, dfe), lambda b, t, e: (e, 0, 1)),
            pl.BlockSpec((None, dfe, D), lambda b, t, e: (e, 0, 0)),
        ],
        out_specs=pl.BlockSpec((nb, tt, D), lambda b, t, e: (b, t, 0)),
        out_shape=jax.ShapeDtypeStruct((B, T, D), F32),
        scratch_shapes=[pltpu.VMEM((nb * tt, D), BF16), pltpu.VMEM((nb * tt, LANES), F32),
                        pltpu.VMEM((nb * tt, LANES), F32),
                        pltpu.VMEM((N_EXPERTS, SUBLANES, nb * tt), F32),
                        pltpu.VMEM((SUBLANES, LANES), F32),
                        pltpu.VMEM((nb * tt, D), F32)],
        compiler_params=_cparams("arbitrary", "arbitrary", "arbitrary"),
        name="moe",
    )(x, g, mod, mod, mod, wr, br, w_exp_in, w_exp_in, w_exp_out)


def _final_kernel(x_ref, g_ref, o_ref):
    x = x_ref[...]
    ms = jnp.mean(x * x, axis=-1, keepdims=True)
    o_ref[...] = (x * lax.rsqrt(ms + EPS)) * g_ref[...]


def _final_norm(x, g):
    B, T, D = x.shape
    nb, tt = _token_blocks(B, T)
    return pl.pallas_call(
        _final_kernel,
        grid=(B // nb, T // tt),
        in_specs=[pl.BlockSpec((nb, tt, D), lambda b, t: (b, t, 0)),
                  pl.BlockSpec((1, D), lambda b, t: (0, 0))],
        out_specs=pl.BlockSpec((nb, tt, D), lambda b, t: (b, t, 0)),
        out_shape=jax.ShapeDtypeStruct((B, T, D), F32),
        compiler_params=_cparams("arbitrary", "arbitrary"),
        name="final_norm",
    )(x, g.reshape(1, D))


def _dup(a):
    return jnp.concatenate([a, a], axis=-1)


def _pad_rows(a, rows):
    return jnp.pad(a, ((0, 0), (0, rows - a.shape[1]), (0, 0)))


def _trunk(x, mods, past, p):
    B, T, D = x.shape
    depth = p["w_in"].shape[0]
    states = [[], [], [], [], []]
    for l in range(depth):
        mod = mods[l]
        P, k_new, v_new, ik_new = _inproj(x, p["g_norm_mix"][l].reshape(1, D), mod, p["w_in"][l])
        if past is None:
            attn = _attention_prompt(P, topk=min(TOPK_MAX, T // 4))
            hist_conf = jnp.zeros((B, HIST_CONF, D_CONF), F32)
            hist_short = jnp.zeros((B, HIST_SHORT, D_SC), F32)
        else:
            cache_k, cache_v, cache_ik, conf_hist, short_hist = [a[l] for a in past]
            n_keys = cache_k.shape[1] + T
            lpad = -(-n_keys // LANES) * LANES
            kk = _pad_rows(_dup(jnp.concatenate([cache_k, k_new], axis=1)), lpad)
            vv = _pad_rows(_dup(jnp.concatenate([cache_v, v_new], axis=1)), lpad)
            ik = _pad_rows(_dup(jnp.concatenate([cache_ik, ik_new], axis=1)), lpad)
            attn = _attention(P, kk, vv, ik, (0, 0, 0),
                              topk=min(TOPK_MAX, n_keys // 4), n_keys=n_keys, causal=False)
            hist_conf = jnp.pad(conf_hist, ((0, 0), (HIST_CONF - (CONF_WIDTH - 1), 0), (0, 0)))
            hist_short = jnp.pad(short_hist, ((0, 0), (HIST_SHORT - (SC_WIDTH - 1), 0), (0, 0)))
        conf, short, st_conf, st_short = _convs(
            P, hist_conf, hist_short, p["w_dw_conf"][l], p["b_dw_conf"][l], p["ln_conf_g"][l],
            p["ln_conf_b"][l], p["w_dw_short"][l])
        x = _merge(attn, conf, short, P, x, mod, p["w_branch"][l], p["w_out"][l])
        if l % 2 == 0:
            x = _ffn(x, p["g_norm_ffn"][l].reshape(1, D), mod, p["w_ffn_in"][l // 2], p["w_ffn_out"][l // 2])
        else:
            x = _moe(x, p["g_norm_ffn"][l].reshape(1, D), mod, p["w_router"][l // 2], p["b_router"][l // 2],
                     p["w_exp_in"][l // 2], p["w_exp_out"][l // 2])
        for i, s in enumerate((k_new, v_new, ik_new, st_conf[:, HIST_CONF - (CONF_WIDTH - 1):],
                               st_short[:, HIST_SHORT - (SC_WIDTH - 1):])):
            states[i].append(s)
    y = _final_norm(x, p["g_final"])
    return y, tuple(jnp.stack(s) for s in states)


def kernel(x_prompt, x_sample, cache_k, cache_v, cache_idx_k, state_conv_conformer, state_conv_short,
           c_prompt, c_sample, w_ada, b_ada, g_norm_mix, g_norm_ffn, w_in, w_dw_conf, b_dw_conf,
           ln_conf_g, ln_conf_b, w_dw_short, w_branch, w_out, w_ffn_in, w_ffn_out, w_router, b_router,
           w_exp_in, w_exp_out, g_final):
    depth, D, _ = w_in.shape
    bp = x_prompt.shape[0]
    mods = _ada(jnp.concatenate([c_prompt, c_sample], axis=0), w_ada, b_ada)
    mods = mods.reshape(depth, -1, 6, 1, D)
    params = {
        "g_norm_mix": g_norm_mix, "g_norm_ffn": g_norm_ffn,
        "w_in": jnp.stack([_rearrange_w_in(w_in[l]) for l in range(depth)]),
        "w_dw_conf": w_dw_conf, "b_dw_conf": b_dw_conf, "ln_conf_g": ln_conf_g, "ln_conf_b": ln_conf_b,
        "w_dw_short": w_dw_short, "w_branch": w_branch.astype(BF16), "w_out": w_out.astype(BF16),
        "w_ffn_in": w_ffn_in.astype(BF16), "w_ffn_out": w_ffn_out.astype(BF16),
        "w_router": w_router, "b_router": b_router,
        "w_exp_in": w_exp_in.astype(BF16), "w_exp_out": w_exp_out.astype(BF16), "g_final": g_final,
    }
    y_p, (pk, pv, pik, pconf, pshort) = _trunk(x_prompt, mods[:, :bp], None, params)
    y_s, (sk, sv, sik, sconf, sshort) = _trunk(
        x_sample, mods[:, bp:],
        (cache_k, cache_v, cache_idx_k, state_conv_conformer, state_conv_short), params)
    return (y_p, y_s, pk, pv, pik, pconf, pshort, sk, sv, sik, sconf, sshort)
```

```python
import functools

import jax
import jax.numpy as jnp
import numpy as np
from jax import lax
from jax.experimental import pallas as pl
from jax.experimental.pallas import tpu as pltpu

CHUNK = 64
N_HEADS = 8
HEAD_DIM = 64
IDX_HEADS = 8
IDX_DIM = 64
TOPK_MAX = 256
D_CONF = 512
CONF_WIDTH = 31
D_SC = 512
SC_WIDTH = 3
N_BRANCH = 3
N_EXPERTS = 8
EPS = 1e-6

LANES = 128
SUBLANES = 8
VMEM_LIMIT = 56 * 1024 * 1024
INT_MIN = np.int32(-2 ** 31)

F32 = jnp.float32
BF16 = jnp.bfloat16

SEC_Q = 0
SEC_IQ = 512
SEC_KK = 1024
SEC_VV = 1152
SEC_IK = 1280
SEC_IW = 1408
SEC_GLU_A = 1536
SEC_GLU_G = 2048
SEC_SC_B = 2560
SEC_SC_C = 3072
SEC_SC_X = 3584
SEC_GATES = 4096
P_WIDTH = 7168
HIST_CONF = 32
HIST_SHORT = 8


def _cparams(*sem):
    return pltpu.CompilerParams(dimension_semantics=sem, vmem_limit_bytes=VMEM_LIMIT)


def _sigmoid(x):
    return 0.5 * jnp.tanh(0.5 * x) + 0.5


def _silu(x):
    return x * _sigmoid(x)


def _token_blocks(B, T, rows=1024):
    if T >= rows:
        assert T % rows == 0
        return 1, rows
    nb = min(B, rows // T)
    assert B % nb == 0
    return nb, T


def _ada_kernel(c_ref, w_ref, b_ref, o_ref):
    c = _silu(c_ref[...]).astype(BF16)
    o_ref[...] = jnp.dot(c, w_ref[...].astype(BF16), preferred_element_type=F32) + b_ref[...]


def _ada(c, w_ada, b_ada):
    depth, d, n = w_ada.shape
    bc = c.shape[0]
    tn = 1536
    assert n % tn == 0
    return pl.pallas_call(
        _ada_kernel,
        grid=(depth, n // tn),
        in_specs=[
            pl.BlockSpec((bc, d), lambda l, j: (0, 0)),
            pl.BlockSpec((None, d, tn), lambda l, j: (l, 0, j)),
            pl.BlockSpec((None, 1, tn), lambda l, j: (l, 0, j)),
        ],
        out_specs=pl.BlockSpec((None, bc, tn), lambda l, j: (l, 0, j)),
        out_shape=jax.ShapeDtypeStruct((depth, bc, n), F32),
        compiler_params=_cparams("arbitrary", "arbitrary"),
        name="ada",
    )(c, w_ada, b_ada.reshape(depth, 1, n))


def _norm_mod(x, g, scale, shift):
    ms = jnp.mean(x * x, axis=-1, keepdims=True)
    xn = (x * lax.rsqrt(ms + EPS)) * g
    h = xn * (1.0 + scale) + shift
    nb, tt, d = x.shape
    return h.reshape(nb * tt, d).astype(BF16)


def _mod_spec(nb, d, which, bmap):
    return pl.BlockSpec((nb, None, 1, d), lambda *a: (bmap(*a), which, 0, 0))


def _inproj_kernel(x_ref, g_ref, sc_ref, sh_ref, w_ref, o_ref, k_ref, v_ref, ik_ref, h_scr):
    j = pl.program_id(2)

    @pl.when(j == 0)
    def _():
        h_scr[...] = _norm_mod(x_ref[...], g_ref[...], sc_ref[...], sh_ref[...])

    nb, tt, tn = o_ref.shape
    res = jnp.dot(h_scr[...], w_ref[...], preferred_element_type=F32)
    o_ref[...] = res.reshape(nb, tt, tn)

    assert SEC_KK // tn == SEC_IK // tn

    @pl.when(j == SEC_KK // tn)
    def _():
        for ref, sec in ((k_ref, SEC_KK), (v_ref, SEC_VV), (ik_ref, SEC_IK)):
            c0 = sec % tn
            ref[...] = res[:, c0:c0 + ref.shape[-1]].reshape(ref.shape)


def _inproj(x, g, mod, w):
    B, T, D = x.shape
    nb, tt = _token_blocks(B, T)
    width = w.shape[1]
    tn = width // 4
    assert width % tn == 0 and tn % LANES == 0
    grid = (B // nb, T // tt, width // tn)

    def row_out(width_):
        return pl.BlockSpec((nb, tt, width_), lambda b, t, j: (b, t, 0))

    return pl.pallas_call(
        _inproj_kernel,
        grid=grid,
        in_specs=[
            pl.BlockSpec((nb, tt, D), lambda b, t, j: (b, t, 0)),
            pl.BlockSpec((1, D), lambda b, t, j: (0, 0)),
            _mod_spec(nb, D, 1, lambda b, t, j: b),
            _mod_spec(nb, D, 0, lambda b, t, j: b),
            pl.BlockSpec((D, tn), lambda b, t, j: (0, j)),
        ],
        out_specs=[pl.BlockSpec((nb, tt, tn), lambda b, t, j: (b, t, j)),
                   row_out(HEAD_DIM), row_out(HEAD_DIM), row_out(IDX_DIM)],
        out_shape=[jax.ShapeDtypeStruct((B, T, width), F32),
                   jax.ShapeDtypeStruct((B, T, HEAD_DIM), F32),
                   jax.ShapeDtypeStruct((B, T, HEAD_DIM), F32),
                   jax.ShapeDtypeStruct((B, T, IDX_DIM), F32)],
        scratch_shapes=[pltpu.VMEM((nb * tt, D), BF16)],
        compiler_params=_cparams("arbitrary", "arbitrary", "arbitrary"),
        name="inproj",
    )(x, g, mod, mod, w)


def _rearrange_w_in(w_in):
    d = w_in.shape[0]
    sizes = (N_HEADS * HEAD_DIM, HEAD_DIM, HEAD_DIM, IDX_HEADS * IDX_DIM, IDX_DIM, IDX_HEADS,
             2 * D_CONF, 3 * D_SC, N_BRANCH * d)
    offs = np.cumsum((0,) + sizes)
    q, k, v, iq, ik, iw, glu, sc, gates = [w_in[:, offs[i]:offs[i + 1]] for i in range(9)]
    iw_pad = jnp.zeros((d, LANES - IDX_HEADS), w_in.dtype)
    out = jnp.concatenate([q, iq, k, k, v, v, ik, ik, iw, iw_pad, glu, sc, gates], axis=1)
    assert out.shape[1] == P_WIDTH
    return out.astype(BF16)


def _stack_heads(x):
    rows = x.shape[0]
    lane = lax.broadcasted_iota(jnp.int32, (rows, LANES), 1)
    lo = lane < HEAD_DIM
    parts = []
    for h in range(N_HEADS):
        slab = x[:, (h // 2) * LANES:(h // 2 + 1) * LANES]
        parts.append(jnp.where(lo if h % 2 == 0 else jnp.logical_not(lo), slab, 0.0))
    return jnp.concatenate(parts, axis=0).astype(BF16)


def _count(mask):
    return jnp.sum(jnp.where(mask, 1.0, 0.0), axis=1, keepdims=True)


def _attn_kernel(q_ref, iq_ref, iw_ref, kk_ref, vv_ref, ik_ref, o_ref, key_scr, sel_scr, p_scr,
                 *, topk, n_keys, causal):
    R = CHUNK
    L = kk_ref.shape[0]
    nt = (((1,), (1,)), ((), ()))
    if causal:
        limit = (pl.program_id(1) + 1) * CHUNK
    else:
        limit = n_keys
    pos = lax.broadcasted_iota(jnp.int32, (R, L), 1)
    admissible = pos < limit

    rel = lax.dot_general(_stack_heads(iq_ref[...]), ik_ref[...].astype(BF16), nt,
                          preferred_element_type=F32)
    iw = iw_ref[...]
    score = jnp.zeros((R, L), F32)
    for h in range(IDX_HEADS):
        r = jnp.maximum(rel[h * R:(h + 1) * R] * (IDX_DIM ** -0.5), 0.0)
        score = score + (iw[:, h:h + 1] * (IDX_HEADS ** -0.5)) * r

    bits = lax.bitcast_convert_type(score, jnp.int32)
    key = jnp.where(bits < 0, bits ^ jnp.int32(0x7FFFFFFF), bits)
    key_scr[...] = jnp.where(admissible, key, INT_MIN)

    def search(i, thr):
        cand = thr + lax.shift_left(jnp.int32(1), 31 - i)
        cnt = _count(key_scr[...] >= cand)
        return jnp.where(cnt >= topk, cand, thr)

    thr = lax.fori_loop(0, 32, search, jnp.full((R, 1), INT_MIN, jnp.int32))
    keys = key_scr[...]
    n_ge = _count(keys >= thr)
    has_tie = jnp.max(n_ge) > topk

    @pl.when(jnp.logical_not(has_tie))
    def _():
        sel_scr[...] = jnp.where(jnp.logical_and(key_scr[...] >= thr, admissible), 1.0, 0.0)

    @pl.when(has_tie)
    def _():
        kk_ = key_scr[...]
        eq = kk_ == thr
        need = topk - _count(kk_ > thr)
        nbits = int(L - 1).bit_length()

        def tie_search(i, d):
            cand = d + lax.shift_left(jnp.int32(1), nbits - 1 - i)
            f = _count(jnp.logical_and(eq, pos < cand))
            return jnp.where(f < need, cand, d)

        d = lax.fori_loop(0, nbits, tie_search, jnp.zeros((R, 1), jnp.int32))
        chosen = jnp.logical_or(kk_ > thr, jnp.logical_and(eq, pos <= d))
        sel_scr[...] = jnp.where(jnp.logical_and(chosen, admissible), 1.0, 0.0)

    logits = lax.dot_general(_stack_heads(q_ref[...]), kk_ref[...].astype(BF16), nt,
                             preferred_element_type=F32) * (HEAD_DIM ** -0.5)
    sel = sel_scr[...] > 0.5
    denoms = []
    for h in range(N_HEADS):
        lg = jnp.where(sel, logits[h * R:(h + 1) * R], -jnp.inf)
        m = jnp.max(lg, axis=1, keepdims=True)
        p = jnp.exp(lg - m)
        denoms.append(jnp.sum(p, axis=1, keepdims=True))
        p_scr[h * R:(h + 1) * R, :] = p.astype(BF16)
    o = jnp.dot(p_scr[...], vv_ref[...].astype(BF16), preferred_element_type=F32)
    lo = lax.broadcasted_iota(jnp.int32, (R, LANES), 1) < HEAD_DIM
    for hp in range(N_HEADS // 2):
        even = o[(2 * hp) * R:(2 * hp + 1) * R] / denoms[2 * hp]
        odd = o[(2 * hp + 1) * R:(2 * hp + 2) * R] / denoms[2 * hp + 1]
        o_ref[:, hp * LANES:(hp + 1) * LANES] = jnp.where(lo, even, odd)


def _attention(P, kk, vv, ik, kv_blocks, *, topk, n_keys, causal):
    B, T, _ = P.shape
    L = kk.shape[1]
    nq = T // CHUNK
    dq = N_HEADS * HEAD_DIM
    kern = functools.partial(_attn_kernel, topk=topk, n_keys=n_keys, causal=causal)
    ck, cv, ci = kv_blocks
    return pl.pallas_call(
        kern,
        grid=(B, nq),
        in_specs=[
            pl.BlockSpec((None, CHUNK, dq), lambda b, j: (b, j, SEC_Q // dq)),
            pl.BlockSpec((None, CHUNK, dq), lambda b, j: (b, j, SEC_IQ // dq)),
            pl.BlockSpec((None, CHUNK, LANES), lambda b, j: (b, j, SEC_IW // LANES)),
            pl.BlockSpec((None, L, LANES), lambda b, j: (b, 0, ck)),
            pl.BlockSpec((None, L, LANES), lambda b, j: (b, 0, cv)),
            pl.BlockSpec((None, L, LANES), lambda b, j: (b, 0, ci)),
        ],
        out_specs=pl.BlockSpec((None, CHUNK, dq), lambda b, j: (b, j, 0)),
        out_shape=jax.ShapeDtypeStruct((B, T, dq), F32),
        scratch_shapes=[
            pltpu.VMEM((CHUNK, L), jnp.int32),
            pltpu.VMEM((CHUNK, L), F32),
            pltpu.VMEM((N_HEADS * CHUNK, L), BF16),
        ],
        compiler_params=_cparams("arbitrary", "arbitrary"),
        name="attn",
    )(P, P, P, kk, vv, ik)


QT = 2 * CHUNK
KB = 256


SUM_ROWS = 64


def _colcount(n_rows, pred):
    acc = jnp.zeros((SUM_ROWS, LANES), F32)
    for k in range(n_rows // SUM_ROWS):
        acc = acc + jnp.where(pred(slice(k * SUM_ROWS, (k + 1) * SUM_ROWS)), 1.0, 0.0)
    return jnp.sum(acc, axis=0, keepdims=True)


PK_ROWS = 128
F32_TINY = float(np.finfo(np.float32).tiny)


def _colcount16(ref, n_rows, pred):
    assert n_rows // PK_ROWS <= 256
    one = jnp.ones((PK_ROWS, LANES), BF16)
    zero = jnp.zeros((PK_ROWS, LANES), BF16)
    acc = zero
    for k in range(n_rows // PK_ROWS):
        acc = acc + jnp.where(pred(ref[k * PK_ROWS:(k + 1) * PK_ROWS, :]), one, zero)
    return jnp.sum(acc.astype(F32), axis=0, keepdims=True)


def _hi_value(m):
    pattern = jnp.where(m < 0, m ^ jnp.int32(0x7FFF), m) & jnp.int32(0xFFFF)
    pattern = jnp.where(jnp.logical_and(m > 0, m < 0x80), jnp.int32(0x80), pattern)
    return lax.bitcast_convert_type(lax.shift_left(pattern, 16), F32).astype(BF16)


def _attn_t_kernel(q_ref, iq_ref, iw_ref, kk_ref, vv_ref, ik_ref, o_ref,
                   key_scr, sel_scr, lg_scr, p_scr, vve_scr, hi_scr, c_scr, *, topk, tile0):
    L = kk_ref.shape[0]
    nkb = L // KB
    nh = N_HEADS
    nt = (((1,), (1,)), ((), ()))
    tn = (((0,), (0,)), ((), ()))
    tile = tile0 + pl.program_id(1)
    lane = lax.broadcasted_iota(jnp.int32, (1, LANES), 1)
    limit = tile * QT + jnp.where(lane < CHUNK, CHUNK, QT)

    w_t = jnp.transpose(iw_ref[...])
    w_rows = [(w_t[h:h + 1, :] * (IDX_HEADS ** -0.5)) * (IDX_DIM ** -0.5) for h in range(nh)]
    qs_i = _stack_heads(iq_ref[...])
    for kb in range(nkb):
        rows = slice(kb * KB, (kb + 1) * KB)
        rel = lax.dot_general(ik_ref[rows, :].astype(BF16), qs_i, nt, preferred_element_type=F32)
        score = jnp.zeros((KB, LANES), F32)
        for h in range(nh):
            score = score + w_rows[h] * jnp.maximum(rel[:, h * QT:(h + 1) * QT], 0.0)
        score = jnp.where(jnp.abs(score) < F32_TINY, 0.0, score)
        bits = lax.bitcast_convert_type(score, jnp.int32)
        key = jnp.where(bits < 0, bits ^ jnp.int32(0x7FFFFFFF), bits)
        pos = kb * KB + lax.broadcasted_iota(jnp.int32, (KB, LANES), 0)
        key_scr[rows, :] = jnp.where(pos < limit, key, INT_MIN)
        trunc = lax.bitcast_convert_type(bits & jnp.int32(-(1 << 16)), F32)
        hi_scr[rows, :] = jnp.where(pos < limit, trunc, -jnp.inf).astype(BF16)
        vve_scr[rows, 0:LANES] = vv_ref[rows, :].astype(BF16)
        vve_scr[rows, LANES:2 * LANES] = jnp.ones((KB, LANES), BF16)

    thr = jnp.full((1, LANES), INT_MIN, jnp.int32)
    if L > topk:
        def stage_hi(i, t):
            cand = t + lax.shift_left(jnp.int32(1), 15 - i)
            cnt = _colcount16(hi_scr, L, lambda a: a >= _hi_value(cand))
            return jnp.where(cnt >= topk, cand, t)

        thr_m = lax.fori_loop(0, 16, stage_hi, jnp.full((1, LANES), -(1 << 15), jnp.int32))
        thr_hi = _hi_value(thr_m)
        need = topk - _colcount16(hi_scr, L, lambda a: a > thr_hi)
        digits = []
        for shift in (8, 0):
            for kb in range(nkb):
                rows = slice(kb * KB, (kb + 1) * KB)
                key = key_scr[rows, :]
                live = lax.shift_right_arithmetic(key, 16) == thr_m
                for prev_shift, prev_digit in digits:
                    live = jnp.logical_and(
                        live, (lax.shift_right_arithmetic(key, prev_shift) & 0xFF) == prev_digit)
                digit = (lax.shift_right_arithmetic(key, shift) & 0xFF).astype(F32)
                c_scr[rows, :] = jnp.where(live, digit, -1.0).astype(BF16)

            def stage_byte(i, t, need=need):
                cand = t + lax.shift_left(jnp.int32(1), 7 - i).astype(F32)
                cnt = _colcount16(c_scr, L, lambda a: a >= cand.astype(BF16))
                return jnp.where(cnt >= need, cand, t)

            d = lax.fori_loop(0, 8, stage_byte, jnp.zeros((1, LANES), F32))
            need = need - _colcount16(c_scr, L, lambda a: a > d.astype(BF16))
            digits.append((shift, d.astype(jnp.int32)))
        thr = lax.shift_left(thr_m, 16) | lax.shift_left(digits[0][1], 8) | digits[1][1]
        thr = jnp.where(limit > topk, thr, INT_MIN)
    has_tie = jnp.max(_colcount(L, lambda r: key_scr[r, :] >= thr)) > topk

    def row_pos(r):
        return r.start + lax.broadcasted_iota(jnp.int32, (r.stop - r.start, LANES), 0)

    @pl.when(jnp.logical_not(has_tie))
    def _():
        for kb in range(nkb):
            r = slice(kb * KB, (kb + 1) * KB)
            chosen = jnp.logical_and(key_scr[r, :] >= thr, row_pos(r) < limit)
            sel_scr[r, :] = jnp.where(chosen, 1.0, 0.0)

    @pl.when(has_tie)
    def _():
        need = topk - _colcount(L, lambda r: key_scr[r, :] > thr)
        nbits = int(L - 1).bit_length()

        def tie_search(i, d):
            cand = d + lax.shift_left(jnp.int32(1), nbits - 1 - i)
            f = _colcount(L, lambda r: jnp.logical_and(key_scr[r, :] == thr, row_pos(r) < cand))
            return jnp.where(f < need, cand, d)

        d = lax.fori_loop(0, nbits, tie_search, jnp.zeros((1, LANES), jnp.int32))
        for kb in range(nkb):
            r = slice(kb * KB, (kb + 1) * KB)
            keys = key_scr[r, :]
            pos = row_pos(r)
            chosen = jnp.logical_or(keys > thr, jnp.logical_and(keys == thr, pos <= d))
            sel_scr[r, :] = jnp.where(jnp.logical_and(chosen, pos < limit), 1.0, 0.0)

    qs = _stack_heads(q_ref[...] * (HEAD_DIM ** -0.5))
    m_run = [jnp.full((1, QT), -jnp.inf, F32) for _ in range(nh)]
    for kb in range(nkb):
        rows = slice(kb * KB, (kb + 1) * KB)
        lg = lax.dot_general(kk_ref[rows, :].astype(BF16), qs, nt, preferred_element_type=F32)
        sel = sel_scr[rows, :] > 0.5
        for h in range(nh):
            lgm = jnp.where(sel, lg[:, h * QT:(h + 1) * QT], -jnp.inf)
            lg_scr[rows, h * QT:(h + 1) * QT] = lgm
            m_run[h] = jnp.maximum(m_run[h], jnp.max(lgm, axis=0, keepdims=True))
    m_all = jnp.concatenate(m_run, axis=1)

    acc = jnp.zeros((nh * QT, 2 * LANES), F32)
    for kb in range(nkb):
        rows = slice(kb * KB, (kb + 1) * KB)
        p_scr[rows, :] = jnp.exp(lg_scr[rows, :] - m_all).astype(BF16)
        acc = acc + lax.dot_general(p_scr[rows, :], vve_scr[rows, :], tn, preferred_element_type=F32)
    on = acc[:, 0:LANES] / acc[:, LANES:2 * LANES]
    lo = lax.broadcasted_iota(jnp.int32, (QT, LANES), 1) < HEAD_DIM
    for hp in range(nh // 2):
        even = on[(2 * hp) * QT:(2 * hp + 1) * QT]
        odd = on[(2 * hp + 1) * QT:(2 * hp + 2) * QT]
        o_ref[:, hp * LANES:(hp + 1) * LANES] = jnp.where(lo, even, odd)


def _attention_prompt(P, *, topk):
    B, T, _ = P.shape
    dq = N_HEADS * HEAD_DIM
    assert T % KB == 0 and QT == LANES
    tiles_per_class = KB // QT
    ck, cv, ci = SEC_KK // LANES, SEC_VV // LANES, SEC_IK // LANES
    outs = []
    for c in range(T // KB):
        L = KB * (c + 1)
        t0 = c * tiles_per_class

        def qspec(width, col, t0=t0):
            return pl.BlockSpec((None, QT, width), lambda b, i: (b, t0 + i, col))

        def kspec(col, L=L):
            return pl.BlockSpec((None, L, LANES), lambda b, i: (b, 0, col))

        outs.append(pl.pallas_call(
            functools.partial(_attn_t_kernel, topk=topk, tile0=t0),
            grid=(B, tiles_per_class),
            in_specs=[qspec(dq, SEC_Q // dq), qspec(dq, SEC_IQ // dq), qspec(LANES, SEC_IW // LANES),
                      kspec(ck), kspec(cv), kspec(ci)],
            out_specs=pl.BlockSpec((None, QT, dq), lambda b, i: (b, i, 0)),
            out_shape=jax.ShapeDtypeStruct((B, tiles_per_class * QT, dq), F32),
            scratch_shapes=[
                pltpu.VMEM((L, LANES), jnp.int32),
                pltpu.VMEM((L, LANES), F32),
                pltpu.VMEM((L, N_HEADS * QT), F32),
                pltpu.VMEM((L, N_HEADS * QT), BF16),
                pltpu.VMEM((L, 2 * LANES), BF16),
                pltpu.VMEM((L, LANES), BF16),
                pltpu.VMEM((L, LANES), BF16),
            ],
            compiler_params=_cparams("arbitrary", "arbitrary"),
            name=f"attn_prompt_L{L}",
        )(P, P, P, P, P, P))
    return jnp.concatenate(outs, axis=1)


def _conv_kernel(a_ref, g_ref, pa_ref, pg_ref, hc_ref, sb_ref, sc_ref, sx_ref, psc_ref, psx_ref,
                 hs_ref, wc_ref, bc_ref, lg_ref, lb_ref, ws_ref,
                 conf_ref, short_ref, stc_ref, sts_ref, u_scr, us_scr, sh_scr, *, rows):
    t = pl.program_id(1)
    TT = a_ref.shape[0]

    @pl.when(t == 0)
    def _():
        u_scr[0:HIST_CONF, :] = hc_ref[...]
        us_scr[0:HIST_SHORT, :] = hs_ref[...]

    @pl.when(t > 0)
    def _():
        u_scr[0:HIST_CONF, :] = pa_ref[...] * _sigmoid(pg_ref[...])
        us_scr[0:HIST_SHORT, :] = psc_ref[...] * psx_ref[...]

    u_scr[HIST_CONF:, :] = a_ref[...] * _sigmoid(g_ref[...])
    us_scr[HIST_SHORT:, :] = sc_ref[...] * sx_ref[...]

    stc_ref[...] = u_scr[TT:TT + HIST_CONF, :]
    sts_ref[...] = us_scr[TT:TT + HIST_SHORT, :]

    off_c = HIST_CONF - (CONF_WIDTH - 1)
    off_s = HIST_SHORT - (SC_WIDTH - 1)

    s = jnp.zeros((TT, D_SC), F32)
    for w in range(SC_WIDTH):
        s = s + us_scr[off_s + w:off_s + w + TT, :] * ws_ref[w:w + 1, :]
    short_ref[...] = (sb_ref[...] * s).astype(BF16)

    for ph in range(SUBLANES):
        span = TT + HIST_CONF - (SUBLANES if ph else 0)
        sh_scr[ph, 0:span, :] = u_scr[ph:ph + span, :]

    for c in range(D_CONF // LANES):
        cs = slice(c * LANES, (c + 1) * LANES)

        def body(r, carry, cs=cs):
            r0 = pl.multiple_of(r * rows, rows)
            acc = jnp.zeros((rows, LANES), F32) + bc_ref[:, cs]
            for ph in range(SUBLANES):
                steps = [a for a in range(HIST_CONF // SUBLANES + 1)
                         if off_c <= ph + a * SUBLANES <= off_c + CONF_WIDTH - 1]
                a0 = steps[0]
                span = rows + (steps[-1] - a0) * SUBLANES
                win = sh_scr[ph, pl.ds(r0 + a0 * SUBLANES, span), cs]
                for a in steps:
                    w = ph + a * SUBLANES - off_c
                    tap = win[(a - a0) * SUBLANES:(a - a0) * SUBLANES + rows]
                    acc = acc + tap * wc_ref[w:w + 1, cs]
            u_scr[pl.ds(r0, rows), cs] = acc
            return carry

        lax.fori_loop(0, TT // rows, body, 0)

    conv = u_scr[0:TT, :]
    mu = jnp.mean(conv, axis=-1, keepdims=True)
    xc = conv - mu
    var = jnp.mean(xc * xc, axis=-1, keepdims=True)
    y = (xc * lax.rsqrt(var + EPS)) * lg_ref[...] + lb_ref[...]
    conf_ref[...] = _silu(y).astype(BF16)


def _convs(P, hist_conf, hist_short, w_dw_conf, b_dw_conf, ln_g, ln_b, w_dw_short):
    B, T, _ = P.shape
    TT = min(T, 512)
    nt = T // TT
    rows = 64
    C = D_CONF

    def sec(off):
        return pl.BlockSpec((None, TT, C), lambda b, t: (b, t, off // C))

    def prev(off, h):
        return pl.BlockSpec((None, h, C), lambda b, t: (b, jnp.maximum(t * (TT // h) - 1, 0), off // C))

    def hist(h):
        return pl.BlockSpec((None, h, C), lambda b, t: (b, 0, 0))

    def vec(r):
        return pl.BlockSpec((r, C), lambda b, t: (0, 0))

    wc = jnp.zeros((32, C), F32).at[:CONF_WIDTH].set(w_dw_conf)
    ws = jnp.zeros((8, C), F32).at[:SC_WIDTH].set(w_dw_short)
    outs = pl.pallas_call(
        functools.partial(_conv_kernel, rows=rows),
        grid=(B, nt),
        in_specs=[
            sec(SEC_GLU_A), sec(SEC_GLU_G), prev(SEC_GLU_A, HIST_CONF), prev(SEC_GLU_G, HIST_CONF),
            hist(HIST_CONF),
            sec(SEC_SC_B), sec(SEC_SC_C), sec(SEC_SC_X), prev(SEC_SC_C, HIST_SHORT),
            prev(SEC_SC_X, HIST_SHORT), hist(HIST_SHORT),
            vec(32), vec(1), vec(1), vec(1), vec(8),
        ],
        out_specs=[
            pl.BlockSpec((None, TT, C), lambda b, t: (b, t, 0)),
            pl.BlockSpec((None, TT, C), lambda b, t: (b, t, 0)),
            pl.BlockSpec((None, HIST_CONF, C), lambda b, t: (b, 0, 0)),
            pl.BlockSpec((None, HIST_SHORT, C), lambda b, t: (b, 0, 0)),
        ],
        out_shape=[
            jax.ShapeDtypeStruct((B, T, C), BF16),
            jax.ShapeDtypeStruct((B, T, C), BF16),
            jax.ShapeDtypeStruct((B, HIST_CONF, C), F32),
            jax.ShapeDtypeStruct((B, HIST_SHORT, C), F32),
        ],
        scratch_shapes=[
            pltpu.VMEM((HIST_CONF + TT, C), F32),
            pltpu.VMEM((HIST_SHORT + TT, C), F32),
            pltpu.VMEM((SUBLANES, HIST_CONF + TT, C), F32),
        ],
        compiler_params=_cparams("arbitrary", "arbitrary"),
        name="convs",
    )(P, P, P, P, hist_conf, P, P, P, P, P, hist_short,
      wc, b_dw_conf.reshape(1, C), ln_g.reshape(1, C), ln_b.reshape(1, C), ws)
    return outs


def _merge_kernel(attn_ref, conf_ref, short_ref, ga_ref, gb_ref, gc_ref, x_ref, gm_ref,
                  wb_ref, wo_ref, o_ref):
    nb, tt, d = x_ref.shape
    n = nb * tt

    def branch(ref, i):
        v = ref[...].reshape(n, ref.shape[-1]).astype(BF16)
        return jnp.dot(v, wb_ref[i], preferred_element_type=F32)

    merged = (_sigmoid(ga_ref[...].reshape(n, d)) * branch(attn_ref, 0)
              + _sigmoid(gb_ref[...].reshape(n, d)) * branch(conf_ref, 1)
              + _sigmoid(gc_ref[...].reshape(n, d)) * branch(short_ref, 2))
    out = jnp.dot(merged.astype(BF16), wo_ref[...], preferred_element_type=F32)
    o_ref[...] = x_ref[...] + gm_ref[...] * out.reshape(nb, tt, d)


def _merge(attn, conf, short, P, x, mod, w_branch, w_out):
    B, T, D = x.shape
    nb, tt = _token_blocks(B, T, 512)
    C = attn.shape[-1]

    def tok(width, col):
        return pl.BlockSpec((nb, tt, width), lambda b, t: (b, t, col))

    g0 = SEC_GATES // D
    return pl.pallas_call(
        _merge_kernel,
        grid=(B // nb, T // tt),
        in_specs=[
            tok(C, 0), tok(C, 0), tok(C, 0),
            tok(D, g0), tok(D, g0 + 1), tok(D, g0 + 2),
            tok(D, 0),
            _mod_spec(nb, D, 2, lambda b, t: b),
            pl.BlockSpec((N_BRANCH, C, D), lambda b, t: (0, 0, 0)),
            pl.BlockSpec((D, D), lambda b, t: (0, 0)),
        ],
        out_specs=tok(D, 0),
        out_shape=jax.ShapeDtypeStruct((B, T, D), F32),
        compiler_params=_cparams("arbitrary", "arbitrary"),
        name="merge",
    )(attn, conf, short, P, P, P, x, mod, w_branch, w_out)


def _ffn_kernel(x_ref, g_ref, sc_ref, sh_ref, gf_ref, wg_ref, wu_ref, wo_ref, o_ref, h_scr, acc_scr):
    f = pl.program_id(2)

    @pl.when(f == 0)
    def _():
        h_scr[...] = _norm_mod(x_ref[...], g_ref[...], sc_ref[...], sh_ref[...])
        acc_scr[...] = jnp.zeros_like(acc_scr)

    h = h_scr[...]
    gate = jnp.dot(h, wg_ref[...], preferred_element_type=F32)
    up = jnp.dot(h, wu_ref[...], preferred_element_type=F32)
    act = (_silu(gate) * up).astype(BF16)
    acc_scr[...] += jnp.dot(act, wo_ref[...], preferred_element_type=F32)

    @pl.when(f == pl.num_programs(2) - 1)
    def _():
        nb, tt, d = x_ref.shape
        o_ref[...] = x_ref[...] + gf_ref[...] * acc_scr[...].reshape(nb, tt, d)


def _ffn(x, g, mod, w_in, w_out):
    B, T, D = x.shape
    nb, tt = _token_blocks(B, T)
    dff = w_out.shape[0]
    tf = dff // 2
    assert tf % LANES == 0
    nf = dff // tf
    return pl.pallas_call(
        _ffn_kernel,
        grid=(B // nb, T // tt, nf),
        in_specs=[
            pl.BlockSpec((nb, tt, D), lambda b, t, f: (b, t, 0)),
            pl.BlockSpec((1, D), lambda b, t, f: (0, 0)),
            _mod_spec(nb, D, 4, lambda b, t, f: b),
            _mod_spec(nb, D, 3, lambda b, t, f: b),
            _mod_spec(nb, D, 5, lambda b, t, f: b),
            pl.BlockSpec((D, tf), lambda b, t, f: (0, f)),
            pl.BlockSpec((D, tf), lambda b, t, f: (0, nf + f)),
            pl.BlockSpec((tf, D), lambda b, t, f: (f, 0)),
        ],
        out_specs=pl.BlockSpec((nb, tt, D), lambda b, t, f: (b, t, 0)),
        out_shape=jax.ShapeDtypeStruct((B, T, D), F32),
        scratch_shapes=[pltpu.VMEM((nb * tt, D), BF16), pltpu.VMEM((nb * tt, D), F32)],
        compiler_params=_cparams("arbitrary", "arbitrary", "arbitrary"),
        name="ffn",
    )(x, g, mod, mod, mod, w_in, w_in, w_out)


MOE_CHUNK = 128


def _moe_kernel(x_ref, g_ref, sc_ref, sh_ref, gf_ref, wr_ref, br_ref, wg_ref, wu_ref, wo_ref,
                o_ref, h_scr, comb_scr, rank_scr, rank_t_scr, cnt_scr, acc_scr):
    e = pl.program_id(2)
    tm = h_scr.shape[0]
    lane = lax.broadcasted_iota(jnp.int32, (tm, LANES), 1)

    @pl.when(e == 0)
    def _():
        h = _norm_mod(x_ref[...], g_ref[...], sc_ref[...], sh_ref[...])
        h_scr[...] = h
        acc_scr[...] = jnp.zeros_like(acc_scr)
        logits = jnp.dot(h, wr_ref[...], preferred_element_type=F32) + br_ref[...]
        logits = jnp.where(lane < N_EXPERTS, logits, -jnp.inf)
        m1 = jnp.max(logits, axis=1, keepdims=True)
        i1 = jnp.min(jnp.where(logits == m1, lane, LANES), axis=1, keepdims=True)
        rest = jnp.where(lane == i1, -jnp.inf, logits)
        m2 = jnp.max(rest, axis=1, keepdims=True)
        i2 = jnp.min(jnp.where(rest == m2, lane, LANES), axis=1, keepdims=True)
        e2 = jnp.exp(m2 - m1)
        w1 = 1.0 / (1.0 + e2)
        w2 = e2 / (1.0 + e2)
        comb = jnp.where(lane == i1, w1, 0.0) + jnp.where(lane == i2, w2, 0.0)
        comb_scr[...] = comb
        routed = comb > 0.0
        before = (lax.broadcasted_iota(jnp.int32, (tm, tm), 1)
                  < lax.broadcasted_iota(jnp.int32, (tm, tm), 0))
        rank = jnp.dot(jnp.where(before, 1.0, 0.0).astype(BF16),
                       jnp.where(routed, 1.0, 0.0).astype(BF16), preferred_element_type=F32)
        rank = jnp.where(routed, rank, -1.0)
        rank_scr[...] = rank
        rank_t = jnp.transpose(rank)
        for ex in range(N_EXPERTS):
            rank_t_scr[ex] = jnp.broadcast_to(rank_t[ex:ex + 1, :], (SUBLANES, tm))
        cnt_scr[...] = jnp.broadcast_to(
            jnp.sum(jnp.where(routed, 1.0, 0.0), axis=0, keepdims=True), (SUBLANES, LANES))

    lane8 = lax.broadcasted_iota(jnp.int32, (SUBLANES, LANES), 1)
    n_rows = jnp.max(jnp.where(lane8 == e, cnt_scr[...], 0.0)).astype(jnp.int32)
    n_chunks = (n_rows + (MOE_CHUNK - 1)) // MOE_CHUNK
    mine = lane == e
    rank_col = jnp.sum(jnp.where(mine, rank_scr[...], 0.0), axis=1, keepdims=True)
    comb_col = jnp.sum(jnp.where(mine, comb_scr[...], 0.0), axis=1, keepdims=True)
    rank_row = rank_t_scr[e][0:1, :]
    slot_sub = lax.broadcasted_iota(jnp.int32, (MOE_CHUNK, tm), 0).astype(F32)
    slot_lane = lax.broadcasted_iota(jnp.int32, (tm, 2 * MOE_CHUNK), 1)
    slot_lane = jnp.where(slot_lane >= MOE_CHUNK, slot_lane - MOE_CHUNK, slot_lane).astype(F32)

    def chunk(c, carry):
        base = (c * MOE_CHUNK).astype(F32)
        gather = jnp.where(rank_row == slot_sub + base, 1.0, 0.0).astype(BF16)
        hs = jnp.dot(gather, h_scr[...], preferred_element_type=F32).astype(BF16)
        gate = jnp.dot(hs, wg_ref[...], preferred_element_type=F32)
        up = jnp.dot(hs, wu_ref[...], preferred_element_type=F32)
        act = (_silu(gate) * up).astype(BF16)
        y = jnp.dot(act, wo_ref[...], preferred_element_type=F32)
        y_hi = y.astype(BF16)
        y_lo = (y - y_hi.astype(F32)).astype(BF16)
        scatter = jnp.where(rank_col == slot_lane + base, 1.0, 0.0).astype(BF16)
        back = jnp.dot(scatter, jnp.concatenate([y_hi, y_lo], axis=0), preferred_element_type=F32)
        acc_scr[...] += comb_col * back
        return carry

    lax.fori_loop(0, n_chunks, chunk, 0)

    @pl.when(e == pl.num_programs(2) - 1)
    def _():
        nb, tt, d = x_ref.shape
        o_ref[...] = x_ref[...] + gf_ref[...] * acc_scr[...].reshape(nb, tt, d)


def _moe(x, g, mod, w_router, b_router, w_exp_in, w_exp_out):
    B, T, D = x.shape
    nb, tt = _token_blocks(B, T)
    ne, dfe, _ = w_exp_out.shape
    assert ne == N_EXPERTS and dfe % LANES == 0
    wr = jnp.zeros((D, LANES), BF16).at[:, :ne].set(w_router.astype(BF16))
    br = jnp.zeros((1, LANES), F32).at[0, :ne].set(b_router)
    return pl.pallas_call(
        _moe_kernel,
        grid=(B // nb, T // tt, ne),
        in_specs=[
            pl.BlockSpec((nb, tt, D), lambda b, t, e: (b, t, 0)),
            pl.BlockSpec((1, D), lambda b, t, e: (0, 0)),
            _mod_spec(nb, D, 4, lambda b, t, e: b),
            _mod_spec(nb, D, 3, lambda b, t, e: b),
            _mod_spec(nb, D, 5, lambda b, t, e: b),
            pl.BlockSpec((D, LANES), lambda b, t, e: (0, 0)),
            pl.BlockSpec((1, LANES), lambda b, t, e: (0, 0)),
            pl.BlockSpec((None, D, dfe), lambda b, t, e: (e, 0, 0)),
            pl.BlockSpec((None, D, dfe), lambda b, t, e: (e, 0, 1)),
            pl.BlockSpec((None, dfe, D), lambda b, t, e: (e, 0, 0)),
        ],
        out_specs=pl.BlockSpec((nb, tt, D), lambda b, t, e: (b, t, 0)),
        out_shape=jax.ShapeDtypeStruct((B, T, D), F32),
        scratch_shapes=[pltpu.VMEM((nb * tt, D), BF16), pltpu.VMEM((nb * tt, LANES), F32),
                        pltpu.VMEM((nb * tt, LANES), F32),
                        pltpu.VMEM((N_EXPERTS, SUBLANES, nb * tt), F32),
                        pltpu.VMEM((SUBLANES, LANES), F32),
                        pltpu.VMEM((nb * tt, D), F32)],
        compiler_params=_cparams("arbitrary", "arbitrary", "arbitrary"),
        name="moe",
    )(x, g, mod, mod, mod, wr, br, w_exp_in, w_exp_in, w_exp_out)


def _final_kernel(x_ref, g_ref, o_ref):
    x = x_ref[...]
    ms = jnp.mean(x * x, axis=-1, keepdims=True)
    o_ref[...] = (x * lax.rsqrt(ms + EPS)) * g_ref[...]


def _final_norm(x, g):
    B, T, D = x.shape
    nb, tt = _token_blocks(B, T)
    return pl.pallas_call(
        _final_kernel,
        grid=(B // nb, T // tt),
        in_specs=[pl.BlockSpec((nb, tt, D), lambda b, t: (b, t, 0)),
                  pl.BlockSpec((1, D), lambda b, t: (0, 0))],
        out_specs=pl.BlockSpec((nb, tt, D), lambda b, t: (b, t, 0)),
        out_shape=jax.ShapeDtypeStruct((B, T, D), F32),
        compiler_params=_cparams("arbitrary", "arbitrary"),
        name="final_norm",
    )(x, g.reshape(1, D))


def _dup(a):
    return jnp.concatenate([a, a], axis=-1)


def _pad_rows(a, rows):
    return jnp.pad(a, ((0, 0), (0, rows - a.shape[1]), (0, 0)))


def _trunk(x, mods, past, p):
    B, T, D = x.shape
    depth = p["w_in"].shape[0]
    states = [[], [], [], [], []]
    for l in range(depth):
        mod = mods[l]
        P, k_new, v_new, ik_new = _inproj(x, p["g_norm_mix"][l].reshape(1, D), mod, p["w_in"][l])
        if past is None:
            attn = _attention_prompt(P, topk=min(TOPK_MAX, T // 4))
            hist_conf = jnp.zeros((B, HIST_CONF, D_CONF), F32)
            hist_short = jnp.zeros((B, HIST_SHORT, D_SC), F32)
        else:
            cache_k, cache_v, cache_ik, conf_hist, short_hist = [a[l] for a in past]
            n_keys = cache_k.shape[1] + T
            lpad = -(-n_keys // LANES) * LANES
            kk = _pad_rows(_dup(jnp.concatenate([cache_k, k_new], axis=1)), lpad)
            vv = _pad_rows(_dup(jnp.concatenate([cache_v, v_new], axis=1)), lpad)
            ik = _pad_rows(_dup(jnp.concatenate([cache_ik, ik_new], axis=1)), lpad)
            attn = _attention(P, kk, vv, ik, (0, 0, 0),
                              topk=min(TOPK_MAX, n_keys // 4), n_keys=n_keys, causal=False)
            hist_conf = jnp.pad(conf_hist, ((0, 0), (HIST_CONF - (CONF_WIDTH - 1), 0), (0, 0)))
            hist_short = jnp.pad(short_hist, ((0, 0), (HIST_SHORT - (SC_WIDTH - 1), 0), (0, 0)))
        conf, short, st_conf, st_short = _convs(
            P, hist_conf, hist_short, p["w_dw_conf"][l], p["b_dw_conf"][l], p["ln_conf_g"][l],
            p["ln_conf_b"][l], p["w_dw_short"][l])
        x = _merge(attn, conf, short, P, x, mod, p["w_branch"][l], p["w_out"][l])
        if l % 2 == 0:
            x = _ffn(x, p["g_norm_ffn"][l].reshape(1, D), mod, p["w_ffn_in"][l // 2], p["w_ffn_out"][l // 2])
        else:
            x = _moe(x, p["g_norm_ffn"][l].reshape(1, D), mod, p["w_router"][l // 2], p["b_router"][l // 2],
                     p["w_exp_in"][l // 2], p["w_exp_out"][l // 2])
        for i, s in enumerate((k_new, v_new, ik_new, st_conf[:, HIST_CONF - (CONF_WIDTH - 1):],
                               st_short[:, HIST_SHORT - (SC_WIDTH - 1):])):
            states[i].append(s)
    y = _final_norm(x, p["g_final"])
    return y, tuple(jnp.stack(s) for s in states)


def kernel(x_prompt, x_sample, cache_k, cache_v, cache_idx_k, state_conv_conformer, state_conv_short,
           c_prompt, c_sample, w_ada, b_ada, g_norm_mix, g_norm_ffn, w_in, w_dw_conf, b_dw_conf,
           ln_conf_g, ln_conf_b, w_dw_short, w_branch, w_out, w_ffn_in, w_ffn_out, w_router, b_router,
           w_exp_in, w_exp_out, g_final):
    depth, D, _ = w_in.shape
    bp = x_prompt.shape[0]
    mods = _ada(jnp.concatenate([c_prompt, c_sample], axis=0), w_ada, b_ada)
    mods = mods.reshape(depth, -1, 6, 1, D)
    params = {
        "g_norm_mix": g_norm_mix, "g_norm_ffn": g_norm_ffn,
        "w_in": jnp.stack([_rearrange_w_in(w_in[l]) for l in range(depth)]),
        "w_dw_conf": w_dw_conf, "b_dw_conf": b_dw_conf, "ln_conf_g": ln_conf_g, "ln_conf_b": ln_conf_b,
        "w_dw_short": w_dw_short, "w_branch": w_branch.astype(BF16), "w_out": w_out.astype(BF16),
        "w_ffn_in": w_ffn_in.astype(BF16), "w_ffn_out": w_ffn_out.astype(BF16),
        "w_router": w_router, "b_router": b_router,
        "w_exp_in": w_exp_in.astype(BF16), "w_exp_out": w_exp_out.astype(BF16), "g_final": g_final,
    }
    y_p, (pk, pv, pik, pconf, pshort) = _trunk(x_prompt, mods[:, :bp], None, params)
    y_s, (sk, sv, sik, sconf, sshort) = _trunk(
        x_sample, mods[:, bp:],
        (cache_k, cache_v, cache_idx_k, state_conv_conformer, state_conv_short), params)
    return (y_p, y_s, pk, pv, pik, pconf, pshort, sk, sv, sik, sconf, sshort)
```

```python
import functools

import jax
import jax.numpy as jnp
import numpy as np
from jax import lax
from jax.experimental import pallas as pl
from jax.experimental.pallas import tpu as pltpu

CHUNK = 64
N_HEADS = 8
HEAD_DIM = 64
IDX_HEADS = 8
IDX_DIM = 64
TOPK_MAX = 256
D_CONF = 512
CONF_WIDTH = 31
D_SC = 512
SC_WIDTH = 3
N_BRANCH = 3
N_EXPERTS = 8
EPS = 1e-6

LANES = 128
SUBLANES = 8
VMEM_LIMIT = 56 * 1024 * 1024
INT_MIN = np.int32(-2 ** 31)

F32 = jnp.float32
BF16 = jnp.bfloat16

SEC_Q = 0
SEC_IQ = 512
SEC_KK = 1024
SEC_VV = 1152
SEC_IK = 1280
SEC_IW = 1408
SEC_GLU_A = 1536
SEC_GLU_G = 2048
SEC_SC_B = 2560
SEC_SC_C = 3072
SEC_SC_X = 3584
SEC_GATES = 4096
P_WIDTH = 7168
HIST_CONF = 32
HIST_SHORT = 8


def _cparams(*sem):
    return pltpu.CompilerParams(dimension_semantics=sem, vmem_limit_bytes=VMEM_LIMIT)


def _sigmoid(x):
    return 0.5 * jnp.tanh(0.5 * x) + 0.5


def _silu(x):
    return x * _sigmoid(x)


def _token_blocks(B, T, rows=1024):
    if T >= rows:
        assert T % rows == 0
        return 1, rows
    nb = min(B, rows // T)
    assert B % nb == 0
    return nb, T


def _ada_kernel(c_ref, w_ref, b_ref, o_ref):
    c = _silu(c_ref[...]).astype(BF16)
    o_ref[...] = jnp.dot(c, w_ref[...].astype(BF16), preferred_element_type=F32) + b_ref[...]


def _ada(c, w_ada, b_ada):
    depth, d, n = w_ada.shape
    bc = c.shape[0]
    tn = 1536
    assert n % tn == 0
    return pl.pallas_call(
        _ada_kernel,
        grid=(depth, n // tn),
        in_specs=[
            pl.BlockSpec((bc, d), lambda l, j: (0, 0)),
            pl.BlockSpec((None, d, tn), lambda l, j: (l, 0, j)),
            pl.BlockSpec((None, 1, tn), lambda l, j: (l, 0, j)),
        ],
        out_specs=pl.BlockSpec((None, bc, tn), lambda l, j: (l, 0, j)),
        out_shape=jax.ShapeDtypeStruct((depth, bc, n), F32),
        compiler_params=_cparams("arbitrary", "arbitrary"),
        name="ada",
    )(c, w_ada, b_ada.reshape(depth, 1, n))


def _norm_mod(x, g, scale, shift):
    ms = jnp.mean(x * x, axis=-1, keepdims=True)
    xn = (x * lax.rsqrt(ms + EPS)) * g
    h = xn * (1.0 + scale) + shift
    nb, tt, d = x.shape
    return h.reshape(nb * tt, d).astype(BF16)


def _mod_spec(nb, d, which, bmap):
    return pl.BlockSpec((nb, None, 1, d), lambda *a: (bmap(*a), which, 0, 0))


def _inproj_kernel(x_ref, g_ref, sc_ref, sh_ref, w_ref, o_ref, k_ref, v_ref, ik_ref, h_scr):
    j = pl.program_id(2)

    @pl.when(j == 0)
    def _():
        h_scr[...] = _norm_mod(x_ref[...], g_ref[...], sc_ref[...], sh_ref[...])

    nb, tt, tn = o_ref.shape
    res = jnp.dot(h_scr[...], w_ref[...], preferred_element_type=F32)
    o_ref[...] = res.reshape(nb, tt, tn)

    assert SEC_KK // tn == SEC_IK // tn

    @pl.when(j == SEC_KK // tn)
    def _():
        for ref, sec in ((k_ref, SEC_KK), (v_ref, SEC_VV), (ik_ref, SEC_IK)):
            c0 = sec % tn
            ref[...] = res[:, c0:c0 + ref.shape[-1]].reshape(ref.shape)


def _inproj(x, g, mod, w):
    B, T, D = x.shape
    nb, tt = _token_blocks(B, T)
    width = w.shape[1]
    tn = width // 4
    assert width % tn == 0 and tn % LANES == 0
    grid = (B // nb, T // tt, width // tn)

    def row_out(width_):
        return pl.BlockSpec((nb, tt, width_), lambda b, t, j: (b, t, 0))

    return pl.pallas_call(
        _inproj_kernel,
        grid=grid,
        in_specs=[
            pl.BlockSpec((nb, tt, D), lambda b, t, j: (b, t, 0)),
            pl.BlockSpec((1, D), lambda b, t, j: (0, 0)),
            _mod_spec(nb, D, 1, lambda b, t, j: b),
            _mod_spec(nb, D, 0, lambda b, t, j: b),
            pl.BlockSpec((D, tn), lambda b, t, j: (0, j)),
        ],
        out_specs=[pl.BlockSpec((nb, tt, tn), lambda b, t, j: (b, t, j)),
                   row_out(HEAD_DIM), row_out(HEAD_DIM), row_out(IDX_DIM)],
        out_shape=[jax.ShapeDtypeStruct((B, T, width), F32),
                   jax.ShapeDtypeStruct((B, T, HEAD_DIM), F32),
                   jax.ShapeDtypeStruct((B, T, HEAD_DIM), F32),
                   jax.ShapeDtypeStruct((B, T, IDX_DIM), F32)],
        scratch_shapes=[pltpu.VMEM((nb * tt, D), BF16)],
        compiler_params=_cparams("arbitrary", "arbitrary", "arbitrary"),
        name="inproj",
    )(x, g, mod, mod, w)


def _rearrange_w_in(w_in):
    d = w_in.shape[0]
    sizes = (N_HEADS * HEAD_DIM, HEAD_DIM, HEAD_DIM, IDX_HEADS * IDX_DIM, IDX_DIM, IDX_HEADS,
             2 * D_CONF, 3 * D_SC, N_BRANCH * d)
    offs = np.cumsum((0,) + sizes)
    q, k, v, iq, ik, iw, glu, sc, gates = [w_in[:, offs[i]:offs[i + 1]] for i in range(9)]
    iw_pad = jnp.zeros((d, LANES - IDX_HEADS), w_in.dtype)
    out = jnp.concatenate([q, iq, k, k, v, v, ik, ik, iw, iw_pad, glu, sc, gates], axis=1)
    assert out.shape[1] == P_WIDTH
    return out.astype(BF16)


def _stack_heads(x):
    rows = x.shape[0]
    lane = lax.broadcasted_iota(jnp.int32, (rows, LANES), 1)
    lo = lane < HEAD_DIM
    parts = []
    for h in range(N_HEADS):
        slab = x[:, (h // 2) * LANES:(h // 2 + 1) * LANES]
        parts.append(jnp.where(lo if h % 2 == 0 else jnp.logical_not(lo), slab, 0.0))
    return jnp.concatenate(parts, axis=0).astype(BF16)


def _count(mask):
    return jnp.sum(jnp.where(mask, 1.0, 0.0), axis=1, keepdims=True)


def _attn_kernel(q_ref, iq_ref, iw_ref, kk_ref, vv_ref, ik_ref, o_ref, key_scr, sel_scr, p_scr,
                 *, topk, n_keys, causal):
    R = CHUNK
    L = kk_ref.shape[0]
    nt = (((1,), (1,)), ((), ()))
    if causal:
        limit = (pl.program_id(1) + 1) * CHUNK
    else:
        limit = n_keys
    pos = lax.broadcasted_iota(jnp.int32, (R, L), 1)
    admissible = pos < limit

    rel = lax.dot_general(_stack_heads(iq_ref[...]), ik_ref[...].astype(BF16), nt,
                          preferred_element_type=F32)
    iw = iw_ref[...]
    score = jnp.zeros((R, L), F32)
    for h in range(IDX_HEADS):
        r = jnp.maximum(rel[h * R:(h + 1) * R] * (IDX_DIM ** -0.5), 0.0)
        score = score + (iw[:, h:h + 1] * (IDX_HEADS ** -0.5)) * r

    bits = lax.bitcast_convert_type(score, jnp.int32)
    key = jnp.where(bits < 0, bits ^ jnp.int32(0x7FFFFFFF), bits)
    key_scr[...] = jnp.where(admissible, key, INT_MIN)

    def search(i, thr):
        cand = thr + lax.shift_left(jnp.int32(1), 31 - i)
        cnt = _count(key_scr[...] >= cand)
        return jnp.where(cnt >= topk, cand, thr)

    thr = lax.fori_loop(0, 32, search, jnp.full((R, 1), INT_MIN, jnp.int32))
    keys = key_scr[...]
    n_ge = _count(keys >= thr)
    has_tie = jnp.max(n_ge) > topk

    @pl.when(jnp.logical_not(has_tie))
    def _():
        sel_scr[...] = jnp.where(jnp.logical_and(key_scr[...] >= thr, admissible), 1.0, 0.0)

    @pl.when(has_tie)
    def _():
        kk_ = key_scr[...]
        eq = kk_ == thr
        need = topk - _count(kk_ > thr)
        nbits = int(L - 1).bit_length()

        def tie_search(i, d):
            cand = d + lax.shift_left(jnp.int32(1), nbits - 1 - i)
            f = _count(jnp.logical_and(eq, pos < cand))
            return jnp.where(f < need, cand, d)

        d = lax.fori_loop(0, nbits, tie_search, jnp.zeros((R, 1), jnp.int32))
        chosen = jnp.logical_or(kk_ > thr, jnp.logical_and(eq, pos <= d))
        sel_scr[...] = jnp.where(jnp.logical_and(chosen, admissible), 1.0, 0.0)

    logits = lax.dot_general(_stack_heads(q_ref[...]), kk_ref[...].astype(BF16), nt,
                             preferred_element_type=F32) * (HEAD_DIM ** -0.5)
    sel = sel_scr[...] > 0.5
    denoms = []
    for h in range(N_HEADS):
        lg = jnp.where(sel, logits[h * R:(h + 1) * R], -jnp.inf)
        m = jnp.max(lg, axis=1, keepdims=True)
        p = jnp.exp(lg - m)
        denoms.append(jnp.sum(p, axis=1, keepdims=True))
        p_scr[h * R:(h + 1) * R, :] = p.astype(BF16)
    o = jnp.dot(p_scr[...], vv_ref[...].astype(BF16), preferred_element_type=F32)
    lo = lax.broadcasted_iota(jnp.int32, (R, LANES), 1) < HEAD_DIM
    for hp in range(N_HEADS // 2):
        even = o[(2 * hp) * R:(2 * hp + 1) * R] / denoms[2 * hp]
        odd = o[(2 * hp + 1) * R:(2 * hp + 2) * R] / denoms[2 * hp + 1]
        o_ref[:, hp * LANES:(hp + 1) * LANES] = jnp.where(lo, even, odd)


def _attention(P, kk, vv, ik, kv_blocks, *, topk, n_keys, causal):
    B, T, _ = P.shape
    L = kk.shape[1]
    nq = T // CHUNK
    dq = N_HEADS * HEAD_DIM
    kern = functools.partial(_attn_kernel, topk=topk, n_keys=n_keys, causal=causal)
    ck, cv, ci = kv_blocks
    return pl.pallas_call(
        kern,
        grid=(B, nq),
        in_specs=[
            pl.BlockSpec((None, CHUNK, dq), lambda b, j: (b, j, SEC_Q // dq)),
            pl.BlockSpec((None, CHUNK, dq), lambda b, j: (b, j, SEC_IQ // dq)),
            pl.BlockSpec((None, CHUNK, LANES), lambda b, j: (b, j, SEC_IW // LANES)),
            pl.BlockSpec((None, L, LANES), lambda b, j: (b, 0, ck)),
            pl.BlockSpec((None, L, LANES), lambda b, j: (b, 0, cv)),
            pl.BlockSpec((None, L, LANES), lambda b, j: (b, 0, ci)),
        ],
        out_specs=pl.BlockSpec((None, CHUNK, dq), lambda b, j: (b, j, 0)),
        out_shape=jax.ShapeDtypeStruct((B, T, dq), F32),
        scratch_shapes=[
            pltpu.VMEM((CHUNK, L), jnp.int32),
            pltpu.VMEM((CHUNK, L), F32),
            pltpu.VMEM((N_HEADS * CHUNK, L), BF16),
        ],
        compiler_params=_cparams("arbitrary", "arbitrary"),
        name="attn",
    )(P, P, P, kk, vv, ik)


QT = 2 * CHUNK
KB = 256


SUM_ROWS = 64


def _colcount(n_rows, pred):
    acc = jnp.zeros((SUM_ROWS, LANES), F32)
    for k in range(n_rows // SUM_ROWS):
        acc = acc + jnp.where(pred(slice(k * SUM_ROWS, (k + 1) * SUM_ROWS)), 1.0, 0.0)
    return jnp.sum(acc, axis=0, keepdims=True)


def _attn_t_kernel(q_ref, iq_ref, iw_ref, kk_ref, vv_ref, ik_ref, o_ref,
                   key_scr, sel_scr, lg_scr, p_scr, vve_scr, *, topk, tile0):
    L = kk_ref.shape[0]
    nkb = L // KB
    nh = N_HEADS
    nt = (((1,), (1,)), ((), ()))
    tn = (((0,), (0,)), ((), ()))
    tile = tile0 + pl.program_id(1)
    lane = lax.broadcasted_iota(jnp.int32, (1, LANES), 1)
    limit = tile * QT + jnp.where(lane < CHUNK, CHUNK, QT)

    w_t = jnp.transpose(iw_ref[...])
    w_rows = [(w_t[h:h + 1, :] * (IDX_HEADS ** -0.5)) * (IDX_DIM ** -0.5) for h in range(nh)]
    qs_i = _stack_heads(iq_ref[...])
    for kb in range(nkb):
        rows = slice(kb * KB, (kb + 1) * KB)
        rel = lax.dot_general(ik_ref[rows, :].astype(BF16), qs_i, nt, preferred_element_type=F32)
        score = jnp.zeros((KB, LANES), F32)
        for h in range(nh):
            score = score + w_rows[h] * jnp.maximum(rel[:, h * QT:(h + 1) * QT], 0.0)
        bits = lax.bitcast_convert_type(score, jnp.int32)
        key = jnp.where(bits < 0, bits ^ jnp.int32(0x7FFFFFFF), bits)
        pos = kb * KB + lax.broadcasted_iota(jnp.int32, (KB, LANES), 0)
        key_scr[rows, :] = jnp.where(pos < limit, key, INT_MIN)
        vve_scr[rows, 0:LANES] = vv_ref[rows, :].astype(BF16)
        vve_scr[rows, LANES:2 * LANES] = jnp.ones((KB, LANES), BF16)

    def search(i, thr):
        cand = thr + lax.shift_left(jnp.int32(1), 31 - i)
        cnt = _colcount(L, lambda r: key_scr[r, :] >= cand)
        return jnp.where(cnt >= topk, cand, thr)

    thr = jnp.full((1, LANES), INT_MIN, jnp.int32)
    if L > topk:
        thr = lax.fori_loop(0, 32, search, thr)
    has_tie = jnp.max(_colcount(L, lambda r: key_scr[r, :] >= thr)) > topk

    def row_pos(r):
        return r.start + lax.broadcasted_iota(jnp.int32, (r.stop - r.start, LANES), 0)

    @pl.when(jnp.logical_not(has_tie))
    def _():
        for kb in range(nkb):
            r = slice(kb * KB, (kb + 1) * KB)
            chosen = jnp.logical_and(key_scr[r, :] >= thr, row_pos(r) < limit)
            sel_scr[r, :] = jnp.where(chosen, 1.0, 0.0)

    @pl.when(has_tie)
    def _():
        need = topk - _colcount(L, lambda r: key_scr[r, :] > thr)
        nbits = int(L - 1).bit_length()

        def tie_search(i, d):
            cand = d + lax.shift_left(jnp.int32(1), nbits - 1 - i)
            f = _colcount(L, lambda r: jnp.logical_and(key_scr[r, :] == thr, row_pos(r) < cand))
            return jnp.where(f < need, cand, d)

        d = lax.fori_loop(0, nbits, tie_search, jnp.zeros((1, LANES), jnp.int32))
        for kb in range(nkb):
            r = slice(kb * KB, (kb + 1) * KB)
            keys = key_scr[r, :]
            pos = row_pos(r)
            chosen = jnp.logical_or(keys > thr, jnp.logical_and(keys == thr, pos <= d))
            sel_scr[r, :] = jnp.where(jnp.logical_and(chosen, pos < limit), 1.0, 0.0)

    qs = _stack_heads(q_ref[...] * (HEAD_DIM ** -0.5))
    m_run = [jnp.full((1, QT), -jnp.inf, F32) for _ in range(nh)]
    for kb in range(nkb):
        rows = slice(kb * KB, (kb + 1) * KB)
        lg = lax.dot_general(kk_ref[rows, :].astype(BF16), qs, nt, preferred_element_type=F32)
        sel = sel_scr[rows, :] > 0.5
        for h in range(nh):
            lgm = jnp.where(sel, lg[:, h * QT:(h + 1) * QT], -jnp.inf)
            lg_scr[rows, h * QT:(h + 1) * QT] = lgm
            m_run[h] = jnp.maximum(m_run[h], jnp.max(lgm, axis=0, keepdims=True))
    m_all = jnp.concatenate(m_run, axis=1)

    acc = jnp.zeros((nh * QT, 2 * LANES), F32)
    for kb in range(nkb):
        rows = slice(kb * KB, (kb + 1) * KB)
        p_scr[rows, :] = jnp.exp(lg_scr[rows, :] - m_all).astype(BF16)
        acc = acc + lax.dot_general(p_scr[rows, :], vve_scr[rows, :], tn, preferred_element_type=F32)
    on = acc[:, 0:LANES] / acc[:, LANES:2 * LANES]
    lo = lax.broadcasted_iota(jnp.int32, (QT, LANES), 1) < HEAD_DIM
    for hp in range(nh // 2):
        even = on[(2 * hp) * QT:(2 * hp + 1) * QT]
        odd = on[(2 * hp + 1) * QT:(2 * hp + 2) * QT]
        o_ref[:, hp * LANES:(hp + 1) * LANES] = jnp.where(lo, even, odd)


def _attention_prompt(P, *, topk):
    B, T, _ = P.shape
    dq = N_HEADS * HEAD_DIM
    assert T % KB == 0 and QT == LANES
    tiles_per_class = KB // QT
    ck, cv, ci = SEC_KK // LANES, SEC_VV // LANES, SEC_IK // LANES
    outs = []
    for c in range(T // KB):
        L = KB * (c + 1)
        t0 = c * tiles_per_class

        def qspec(width, col, t0=t0):
            return pl.BlockSpec((None, QT, width), lambda b, i: (b, t0 + i, col))

        def kspec(col, L=L):
            return pl.BlockSpec((None, L, LANES), lambda b, i: (b, 0, col))

        outs.append(pl.pallas_call(
            functools.partial(_attn_t_kernel, topk=topk, tile0=t0),
            grid=(B, tiles_per_class),
            in_specs=[qspec(dq, SEC_Q // dq), qspec(dq, SEC_IQ // dq), qspec(LANES, SEC_IW // LANES),
                      kspec(ck), kspec(cv), kspec(ci)],
            out_specs=pl.BlockSpec((None, QT, dq), lambda b, i: (b, i, 0)),
            out_shape=jax.ShapeDtypeStruct((B, tiles_per_class * QT, dq), F32),
            scratch_shapes=[
                pltpu.VMEM((L, LANES), jnp.int32),
                pltpu.VMEM((L, LANES), F32),
                pltpu.VMEM((L, N_HEADS * QT), F32),
                pltpu.VMEM((L, N_HEADS * QT), BF16),
                pltpu.VMEM((L, 2 * LANES), BF16),
            ],
            compiler_params=_cparams("arbitrary", "arbitrary"),
            name=f"attn_prompt_L{L}",
        )(P, P, P, P, P, P))
    return jnp.concatenate(outs, axis=1)


def _conv_kernel(a_ref, g_ref, pa_ref, pg_ref, hc_ref, sb_ref, sc_ref, sx_ref, psc_ref, psx_ref,
                 hs_ref, wc_ref, bc_ref, lg_ref, lb_ref, ws_ref,
                 conf_ref, short_ref, stc_ref, sts_ref, u_scr, us_scr, sh_scr, *, rows):
    t = pl.program_id(1)
    TT = a_ref.shape[0]

    @pl.when(t == 0)
    def _():
        u_scr[0:HIST_CONF, :] = hc_ref[...]
        us_scr[0:HIST_SHORT, :] = hs_ref[...]

    @pl.when(t > 0)
    def _():
        u_scr[0:HIST_CONF, :] = pa_ref[...] * _sigmoid(pg_ref[...])
        us_scr[0:HIST_SHORT, :] = psc_ref[...] * psx_ref[...]

    u_scr[HIST_CONF:, :] = a_ref[...] * _sigmoid(g_ref[...])
    us_scr[HIST_SHORT:, :] = sc_ref[...] * sx_ref[...]

    stc_ref[...] = u_scr[TT:TT + HIST_CONF, :]
    sts_ref[...] = us_scr[TT:TT + HIST_SHORT, :]

    off_c = HIST_CONF - (CONF_WIDTH - 1)
    off_s = HIST_SHORT - (SC_WIDTH - 1)

    s = jnp.zeros((TT, D_SC), F32)
    for w in range(SC_WIDTH):
        s = s + us_scr[off_s + w:off_s + w + TT, :] * ws_ref[w:w + 1, :]
    short_ref[...] = (sb_ref[...] * s).astype(BF16)

    for ph in range(SUBLANES):
        span = TT + HIST_CONF - (SUBLANES if ph else 0)
        sh_scr[ph, 0:span, :] = u_scr[ph:ph + span, :]

    for c in range(D_CONF // LANES):
        cs = slice(c * LANES, (c + 1) * LANES)

        def body(r, carry, cs=cs):
            r0 = pl.multiple_of(r * rows, rows)
            acc = jnp.zeros((rows, LANES), F32) + bc_ref[:, cs]
            for ph in range(SUBLANES):
                steps = [a for a in range(HIST_CONF // SUBLANES + 1)
                         if off_c <= ph + a * SUBLANES <= off_c + CONF_WIDTH - 1]
                a0 = steps[0]
                span = rows + (steps[-1] - a0) * SUBLANES
                win = sh_scr[ph, pl.ds(r0 + a0 * SUBLANES, span), cs]
                for a in steps:
                    w = ph + a * SUBLANES - off_c
                    tap = win[(a - a0) * SUBLANES:(a - a0) * SUBLANES + rows]
                    acc = acc + tap * wc_ref[w:w + 1, cs]
            u_scr[pl.ds(r0, rows), cs] = acc
            return carry

        lax.fori_loop(0, TT // rows, body, 0)

    conv = u_scr[0:TT, :]
    mu = jnp.mean(conv, axis=-1, keepdims=True)
    xc = conv - mu
    var = jnp.mean(xc * xc, axis=-1, keepdims=True)
    y = (xc * lax.rsqrt(var + EPS)) * lg_ref[...] + lb_ref[...]
    conf_ref[...] = _silu(y).astype(BF16)


def _convs(P, hist_conf, hist_short, w_dw_conf, b_dw_conf, ln_g, ln_b, w_dw_short):
    B, T, _ = P.shape
    TT = min(T, 512)
    nt = T // TT
    rows = 64
    C = D_CONF

    def sec(off):
        return pl.BlockSpec((None, TT, C), lambda b, t: (b, t, off // C))

    def prev(off, h):
        return pl.BlockSpec((None, h, C), lambda b, t: (b, jnp.maximum(t * (TT // h) - 1, 0), off // C))

    def hist(h):
        return pl.BlockSpec((None, h, C), lambda b, t: (b, 0, 0))

    def vec(r):
        return pl.BlockSpec((r, C), lambda b, t: (0, 0))

    wc = jnp.zeros((32, C), F32).at[:CONF_WIDTH].set(w_dw_conf)
    ws = jnp.zeros((8, C), F32).at[:SC_WIDTH].set(w_dw_short)
    outs = pl.pallas_call(
        functools.partial(_conv_kernel, rows=rows),
        grid=(B, nt),
        in_specs=[
            sec(SEC_GLU_A), sec(SEC_GLU_G), prev(SEC_GLU_A, HIST_CONF), prev(SEC_GLU_G, HIST_CONF),
            hist(HIST_CONF),
            sec(SEC_SC_B), sec(SEC_SC_C), sec(SEC_SC_X), prev(SEC_SC_C, HIST_SHORT),
            prev(SEC_SC_X, HIST_SHORT), hist(HIST_SHORT),
            vec(32), vec(1), vec(1), vec(1), vec(8),
        ],
        out_specs=[
            pl.BlockSpec((None, TT, C), lambda b, t: (b, t, 0)),
            pl.BlockSpec((None, TT, C), lambda b, t: (b, t, 0)),
            pl.BlockSpec((None, HIST_CONF, C), lambda b, t: (b, 0, 0)),
            pl.BlockSpec((None, HIST_SHORT, C), lambda b, t: (b, 0, 0)),
        ],
        out_shape=[
            jax.ShapeDtypeStruct((B, T, C), BF16),
            jax.ShapeDtypeStruct((B, T, C), BF16),
            jax.ShapeDtypeStruct((B, HIST_CONF, C), F32),
            jax.ShapeDtypeStruct((B, HIST_SHORT, C), F32),
        ],
        scratch_shapes=[
            pltpu.VMEM((HIST_CONF + TT, C), F32),
            pltpu.VMEM((HIST_SHORT + TT, C), F32),
            pltpu.VMEM((SUBLANES, HIST_CONF + TT, C), F32),
        ],
        compiler_params=_cparams("arbitrary", "arbitrary"),
        name="convs",
    )(P, P, P, P, hist_conf, P, P, P, P, P, hist_short,
      wc, b_dw_conf.reshape(1, C), ln_g.reshape(1, C), ln_b.reshape(1, C), ws)
    return outs


def _merge_kernel(attn_ref, conf_ref, short_ref, ga_ref, gb_ref, gc_ref, x_ref, gm_ref,
                  wb_ref, wo_ref, o_ref):
    nb, tt, d = x_ref.shape
    n = nb * tt

    def branch(ref, i):
        v = ref[...].reshape(n, ref.shape[-1]).astype(BF16)
        return jnp.dot(v, wb_ref[i], preferred_element_type=F32)

    merged = (_sigmoid(ga_ref[...].reshape(n, d)) * branch(attn_ref, 0)
              + _sigmoid(gb_ref[...].reshape(n, d)) * branch(conf_ref, 1)
              + _sigmoid(gc_ref[...].reshape(n, d)) * branch(short_ref, 2))
    out = jnp.dot(merged.astype(BF16), wo_ref[...], preferred_element_type=F32)
    o_ref[...] = x_ref[...] + gm_ref[...] * out.reshape(nb, tt, d)


def _merge(attn, conf, short, P, x, mod, w_branch, w_out):
    B, T, D = x.shape
    nb, tt = _token_blocks(B, T, 512)
    C = attn.shape[-1]

    def tok(width, col):
        return pl.BlockSpec((nb, tt, width), lambda b, t: (b, t, col))

    g0 = SEC_GATES // D
    return pl.pallas_call(
        _merge_kernel,
        grid=(B // nb, T // tt),
        in_specs=[
            tok(C, 0), tok(C, 0), tok(C, 0),
            tok(D, g0), tok(D, g0 + 1), tok(D, g0 + 2),
            tok(D, 0),
            _mod_spec(nb, D, 2, lambda b, t: b),
            pl.BlockSpec((N_BRANCH, C, D), lambda b, t: (0, 0, 0)),
            pl.BlockSpec((D, D), lambda b, t: (0, 0)),
        ],
        out_specs=tok(D, 0),
        out_shape=jax.ShapeDtypeStruct((B, T, D), F32),
        compiler_params=_cparams("arbitrary", "arbitrary"),
        name="merge",
    )(attn, conf, short, P, P, P, x, mod, w_branch, w_out)


def _ffn_kernel(x_ref, g_ref, sc_ref, sh_ref, gf_ref, wg_ref, wu_ref, wo_ref, o_ref, h_scr, acc_scr):
    f = pl.program_id(2)

    @pl.when(f == 0)
    def _():
        h_scr[...] = _norm_mod(x_ref[...], g_ref[...], sc_ref[...], sh_ref[...])
        acc_scr[...] = jnp.zeros_like(acc_scr)

    h = h_scr[...]
    gate = jnp.dot(h, wg_ref[...], preferred_element_type=F32)
    up = jnp.dot(h, wu_ref[...], preferred_element_type=F32)
    act = (_silu(gate) * up).astype(BF16)
    acc_scr[...] += jnp.dot(act, wo_ref[...], preferred_element_type=F32)

    @pl.when(f == pl.num_programs(2) - 1)
    def _():
        nb, tt, d = x_ref.shape
        o_ref[...] = x_ref[...] + gf_ref[...] * acc_scr[...].reshape(nb, tt, d)


def _ffn(x, g, mod, w_in, w_out):
    B, T, D = x.shape
    nb, tt = _token_blocks(B, T)
    dff = w_out.shape[0]
    tf = dff // 2
    assert tf % LANES == 0
    nf = dff // tf
    return pl.pallas_call(
        _ffn_kernel,
        grid=(B // nb, T // tt, nf),
        in_specs=[
            pl.BlockSpec((nb, tt, D), lambda b, t, f: (b, t, 0)),
            pl.BlockSpec((1, D), lambda b, t, f: (0, 0)),
            _mod_spec(nb, D, 4, lambda b, t, f: b),
            _mod_spec(nb, D, 3, lambda b, t, f: b),
            _mod_spec(nb, D, 5, lambda b, t, f: b),
            pl.BlockSpec((D, tf), lambda b, t, f: (0, f)),
            pl.BlockSpec((D, tf), lambda b, t, f: (0, nf + f)),
            pl.BlockSpec((tf, D), lambda b, t, f: (f, 0)),
        ],
        out_specs=pl.BlockSpec((nb, tt, D), lambda b, t, f: (b, t, 0)),
        out_shape=jax.ShapeDtypeStruct((B, T, D), F32),
        scratch_shapes=[pltpu.VMEM((nb * tt, D), BF16), pltpu.VMEM((nb * tt, D), F32)],
        compiler_params=_cparams("arbitrary", "arbitrary", "arbitrary"),
        name="ffn",
    )(x, g, mod, mod, mod, w_in, w_in, w_out)


MOE_CHUNK = 128


def _moe_kernel(x_ref, g_ref, sc_ref, sh_ref, gf_ref, wr_ref, br_ref, wg_ref, wu_ref, wo_ref,
                o_ref, h_scr, comb_scr, rank_scr, rank_t_scr, cnt_scr, acc_scr):
    e = pl.program_id(2)
    tm = h_scr.shape[0]
    lane = lax.broadcasted_iota(jnp.int32, (tm, LANES), 1)

    @pl.when(e == 0)
    def _():
        h = _norm_mod(x_ref[...], g_ref[...], sc_ref[...], sh_ref[...])
        h_scr[...] = h
        acc_scr[...] = jnp.zeros_like(acc_scr)
        logits = jnp.dot(h, wr_ref[...], preferred_element_type=F32) + br_ref[...]
        logits = jnp.where(lane < N_EXPERTS, logits, -jnp.inf)
        m1 = jnp.max(logits, axis=1, keepdims=True)
        i1 = jnp.min(jnp.where(logits == m1, lane, LANES), axis=1, keepdims=True)
        rest = jnp.where(lane == i1, -jnp.inf, logits)
        m2 = jnp.max(rest, axis=1, keepdims=True)
        i2 = jnp.min(jnp.where(rest == m2, lane, LANES), axis=1, keepdims=True)
        e2 = jnp.exp(m2 - m1)
        w1 = 1.0 / (1.0 + e2)
        w2 = e2 / (1.0 + e2)
        comb = jnp.where(lane == i1, w1, 0.0) + jnp.where(lane == i2, w2, 0.0)
        comb_scr[...] = comb
        routed = comb > 0.0
        before = (lax.broadcasted_iota(jnp.int32, (tm, tm), 1)
                  < lax.broadcasted_iota(jnp.int32, (tm, tm), 0))
        rank = jnp.dot(jnp.where(before, 1.0, 0.0).astype(BF16),
                       jnp.where(routed, 1.0, 0.0).astype(BF16), preferred_element_type=F32)
        rank = jnp.where(routed, rank, -1.0)
        rank_scr[...] = rank
        rank_t = jnp.transpose(rank)
        for ex in range(N_EXPERTS):
            rank_t_scr[ex] = jnp.broadcast_to(rank_t[ex:ex + 1, :], (SUBLANES, tm))
        cnt_scr[...] = jnp.broadcast_to(
            jnp.sum(jnp.where(routed, 1.0, 0.0), axis=0, keepdims=True), (SUBLANES, LANES))

    lane8 = lax.broadcasted_iota(jnp.int32, (SUBLANES, LANES), 1)
    n_rows = jnp.max(jnp.where(lane8 == e, cnt_scr[...], 0.0)).astype(jnp.int32)
    n_chunks = (n_rows + (MOE_CHUNK - 1)) // MOE_CHUNK
    mine = lane == e
    rank_col = jnp.sum(jnp.where(mine, rank_scr[...], 0.0), axis=1, keepdims=True)
    comb_col = jnp.sum(jnp.where(mine, comb_scr[...], 0.0), axis=1, keepdims=True)
    rank_row = rank_t_scr[e][0:1, :]
    slot_sub = lax.broadcasted_iota(jnp.int32, (MOE_CHUNK, tm), 0).astype(F32)
    slot_lane = lax.broadcasted_iota(jnp.int32, (tm, 2 * MOE_CHUNK), 1)
    slot_lane = jnp.where(slot_lane >= MOE_CHUNK, slot_lane - MOE_CHUNK, slot_lane).astype(F32)

    def chunk(c, carry):
        base = (c * MOE_CHUNK).astype(F32)
        gather = jnp.where(rank_row == slot_sub + base, 1.0, 0.0).astype(BF16)
        hs = jnp.dot(gather, h_scr[...], preferred_element_type=F32).astype(BF16)
        gate = jnp.dot(hs, wg_ref[...], preferred_element_type=F32)
        up = jnp.dot(hs, wu_ref[...], preferred_element_type=F32)
        act = (_silu(gate) * up).astype(BF16)
        y = jnp.dot(act, wo_ref[...], preferred_element_type=F32)
        y_hi = y.astype(BF16)
        y_lo = (y - y_hi.astype(F32)).astype(BF16)
        scatter = jnp.where(rank_col == slot_lane + base, 1.0, 0.0).astype(BF16)
        back = jnp.dot(scatter, jnp.concatenate([y_hi, y_lo], axis=0), preferred_element_type=F32)
        acc_scr[...] += comb_col * back
        return carry

    lax.fori_loop(0, n_chunks, chunk, 0)

    @pl.when(e == pl.num_programs(2) - 1)
    def _():
        nb, tt, d = x_ref.shape
        o_ref[...] = x_ref[...] + gf_ref[...] * acc_scr[...].reshape(nb, tt, d)


def _moe(x, g, mod, w_router, b_router, w_exp_in, w_exp_out):
    B, T, D = x.shape
    nb, tt = _token_blocks(B, T)
    ne, dfe, _ = w_exp_out.shape
    assert ne == N_EXPERTS and dfe % LANES == 0
    wr = jnp.zeros((D, LANES), BF16).at[:, :ne].set(w_router.astype(BF16))
    br = jnp.zeros((1, LANES), F32).at[0, :ne].set(b_router)
    return pl.pallas_call(
        _moe_kernel,
        grid=(B // nb, T // tt, ne),
        in_specs=[
            pl.BlockSpec((nb, tt, D), lambda b, t, e: (b, t, 0)),
            pl.BlockSpec((1, D), lambda b, t, e: (0, 0)),
            _mod_spec(nb, D, 4, lambda b, t, e: b),
            _mod_spec(nb, D, 3, lambda b, t, e: b),
            _mod_spec(nb, D, 5, lambda b, t, e: b),
            pl.BlockSpec((D, LANES), lambda b, t, e: (0, 0)),
            pl.BlockSpec((1, LANES), lambda b, t, e: (0, 0)),
            pl.BlockSpec((None, D, dfe), lambda b, t, e: (e, 0, 0)),
            pl.BlockSpec((None, D, dfe), lambda b, t, e: (e, 0, 1)),
            pl.BlockSpec((None, dfe, D), lambda b, t, e: (e, 0, 0)),
        ],
        out_specs=pl.BlockSpec((nb, tt, D), lambda b, t, e: (b, t, 0)),
        out_shape=jax.ShapeDtypeStruct((B, T, D), F32),
        scratch_shapes=[pltpu.VMEM((nb * tt, D), BF16), pltpu.VMEM((nb * tt, LANES), F32),
                        pltpu.VMEM((nb * tt, LANES), F32),
                        pltpu.VMEM((N_EXPERTS, SUBLANES, nb * tt), F32),
                        pltpu.VMEM((SUBLANES, LANES), F32),
                        pltpu.VMEM((nb * tt, D), F32)],
        compiler_params=_cparams("arbitrary", "arbitrary", "arbitrary"),
        name="moe",
    )(x, g, mod, mod, mod, wr, br, w_exp_in, w_exp_in, w_exp_out)


def _final_kernel(x_ref, g_ref, o_ref):
    x = x_ref[...]
    ms = jnp.mean(x * x, axis=-1, keepdims=True)
    o_ref[...] = (x * lax.rsqrt(ms + EPS)) * g_ref[...]


def _final_norm(x, g):
    B, T, D = x.shape
    nb, tt = _token_blocks(B, T)
    return pl.pallas_call(
        _final_kernel,
        grid=(B // nb, T // tt),
        in_specs=[pl.BlockSpec((nb, tt, D), lambda b, t: (b, t, 0)),
                  pl.BlockSpec((1, D), lambda b, t: (0, 0))],
        out_specs=pl.BlockSpec((nb, tt, D), lambda b, t: (b, t, 0)),
        out_shape=jax.ShapeDtypeStruct((B, T, D), F32),
        compiler_params=_cparams("arbitrary", "arbitrary"),
        name="final_norm",
    )(x, g.reshape(1, D))


def _dup(a):
    return jnp.concatenate([a, a], axis=-1)


def _pad_rows(a, rows):
    return jnp.pad(a, ((0, 0), (0, rows - a.shape[1]), (0, 0)))


def _trunk(x, mods, past, p):
    B, T, D = x.shape
    depth = p["w_in"].shape[0]
    states = [[], [], [], [], []]
    for l in range(depth):
        mod = mods[l]
        P, k_new, v_new, ik_new = _inproj(x, p["g_norm_mix"][l].reshape(1, D), mod, p["w_in"][l])
        if past is None:
            attn = _attention_prompt(P, topk=min(TOPK_MAX, T // 4))
            hist_conf = jnp.zeros((B, HIST_CONF, D_CONF), F32)
            hist_short = jnp.zeros((B, HIST_SHORT, D_SC), F32)
        else:
            cache_k, cache_v, cache_ik, conf_hist, short_hist = [a[l] for a in past]
            n_keys = cache_k.shape[1] + T
            lpad = -(-n_keys // LANES) * LANES
            kk = _pad_rows(_dup(jnp.concatenate([cache_k, k_new], axis=1)), lpad)
            vv = _pad_rows(_dup(jnp.concatenate([cache_v, v_new], axis=1)), lpad)
            ik = _pad_rows(_dup(jnp.concatenate([cache_ik, ik_new], axis=1)), lpad)
            attn = _attention(P, kk, vv, ik, (0, 0, 0),
                              topk=min(TOPK_MAX, n_keys // 4), n_keys=n_keys, causal=False)
            hist_conf = jnp.pad(conf_hist, ((0, 0), (HIST_CONF - (CONF_WIDTH - 1), 0), (0, 0)))
            hist_short = jnp.pad(short_hist, ((0, 0), (HIST_SHORT - (SC_WIDTH - 1), 0), (0, 0)))
        conf, short, st_conf, st_short = _convs(
            P, hist_conf, hist_short, p["w_dw_conf"][l], p["b_dw_conf"][l], p["ln_conf_g"][l],
            p["ln_conf_b"][l], p["w_dw_short"][l])
        x = _merge(attn, conf, short, P, x, mod, p["w_branch"][l], p["w_out"][l])
        if l % 2 == 0:
            x = _ffn(x, p["g_norm_ffn"][l].reshape(1, D), mod, p["w_ffn_in"][l // 2], p["w_ffn_out"][l // 2])
        else:
            x = _moe(x, p["g_norm_ffn"][l].reshape(1, D), mod, p["w_router"][l // 2], p["b_router"][l // 2],
                     p["w_exp_in"][l // 2], p["w_exp_out"][l // 2])
        for i, s in enumerate((k_new, v_new, ik_new, st_conf[:, HIST_CONF - (CONF_WIDTH - 1):],
                               st_short[:, HIST_SHORT - (SC_WIDTH - 1):])):
            states[i].append(s)
    y = _final_norm(x, p["g_final"])
    return y, tuple(jnp.stack(s) for s in states)


def kernel(x_prompt, x_sample, cache_k, cache_v, cache_idx_k, state_conv_conformer, state_conv_short,
           c_prompt, c_sample, w_ada, b_ada, g_norm_mix, g_norm_ffn, w_in, w_dw_conf, b_dw_conf,
           ln_conf_g, ln_conf_b, w_dw_short, w_branch, w_out, w_ffn_in, w_ffn_out, w_router, b_router,
           w_exp_in, w_exp_out, g_final):
    depth, D, _ = w_in.shape
    bp = x_prompt.shape[0]
    mods = _ada(jnp.concatenate([c_prompt, c_sample], axis=0), w_ada, b_ada)
    mods = mods.reshape(depth, -1, 6, 1, D)
    params = {
        "g_norm_mix": g_norm_mix, "g_norm_ffn": g_norm_ffn,
        "w_in": jnp.stack([_rearrange_w_in(w_in[l]) for l in range(depth)]),
        "w_dw_conf": w_dw_conf, "b_dw_conf": b_dw_conf, "ln_conf_g": ln_conf_g, "ln_conf_b": ln_conf_b,
        "w_dw_short": w_dw_short, "w_branch": w_branch.astype(BF16), "w_out": w_out.astype(BF16),
        "w_ffn_in": w_ffn_in.astype(BF16), "w_ffn_out": w_ffn_out.astype(BF16),
        "w_router": w_router, "b_router": b_router,
        "w_exp_in": w_exp_in.astype(BF16), "w_exp_out": w_exp_out.astype(BF16), "g_final": g_final,
    }
    y_p, (pk, pv, pik, pconf, pshort) = _trunk(x_prompt, mods[:, :bp], None, params)
    y_s, (sk, sv, sik, sconf, sshort) = _trunk(
        x_sample, mods[:, bp:],
        (cache_k, cache_v, cache_idx_k, state_conv_conformer, state_conv_short), params)
    return (y_p, y_s, pk, pv, pik, pconf, pshort, sk, sv, sik, sconf, sshort)
```

```python
import functools

import jax
import jax.numpy as jnp
import numpy as np
from jax import lax
from jax.experimental import pallas as pl
from jax.experimental.pallas import tpu as pltpu

CHUNK = 64
N_HEADS = 8
HEAD_DIM = 64
IDX_HEADS = 8
IDX_DIM = 64
TOPK_MAX = 256
D_CONF = 512
CONF_WIDTH = 31
D_SC = 512
SC_WIDTH = 3
N_BRANCH = 3
N_EXPERTS = 8
EPS = 1e-6

LANES = 128
SUBLANES = 8
VMEM_LIMIT = 56 * 1024 * 1024
INT_MIN = np.int32(-2 ** 31)

F32 = jnp.float32
BF16 = jnp.bfloat16

SEC_Q = 0
SEC_IQ = 512
SEC_KK = 1024
SEC_VV = 1152
SEC_IK = 1280
SEC_IW = 1408
SEC_GLU_A = 1536
SEC_GLU_G = 2048
SEC_SC_B = 2560
SEC_SC_C = 3072
SEC_SC_X = 3584
SEC_GATES = 4096
P_WIDTH = 7168
HIST_CONF = 32
HIST_SHORT = 8


def _cparams(*sem):
    return pltpu.CompilerParams(dimension_semantics=sem, vmem_limit_bytes=VMEM_LIMIT)


def _sigmoid(x):
    return 0.5 * jnp.tanh(0.5 * x) + 0.5


def _silu(x):
    return x * _sigmoid(x)


def _token_blocks(B, T, rows=1024):
    if T >= rows:
        assert T % rows == 0
        return 1, rows
    nb = min(B, rows // T)
    assert B % nb == 0
    return nb, T


def _ada_kernel(c_ref, w_ref, b_ref, o_ref):
    c = _silu(c_ref[...]).astype(BF16)
    o_ref[...] = jnp.dot(c, w_ref[...].astype(BF16), preferred_element_type=F32) + b_ref[...]


def _ada(c, w_ada, b_ada):
    depth, d, n = w_ada.shape
    bc = c.shape[0]
    tn = 1536
    assert n % tn == 0
    return pl.pallas_call(
        _ada_kernel,
        grid=(depth, n // tn),
        in_specs=[
            pl.BlockSpec((bc, d), lambda l, j: (0, 0)),
            pl.BlockSpec((None, d, tn), lambda l, j: (l, 0, j)),
            pl.BlockSpec((None, 1, tn), lambda l, j: (l, 0, j)),
        ],
        out_specs=pl.BlockSpec((None, bc, tn), lambda l, j: (l, 0, j)),
        out_shape=jax.ShapeDtypeStruct((depth, bc, n), F32),
        compiler_params=_cparams("arbitrary", "arbitrary"),
        name="ada",
    )(c, w_ada, b_ada.reshape(depth, 1, n))


def _norm_mod(x, g, scale, shift):
    ms = jnp.mean(x * x, axis=-1, keepdims=True)
    xn = (x * lax.rsqrt(ms + EPS)) * g
    h = xn * (1.0 + scale) + shift
    nb, tt, d = x.shape
    return h.reshape(nb * tt, d).astype(BF16)


def _mod_spec(nb, d, which, bmap):
    return pl.BlockSpec((nb, None, 1, d), lambda *a: (bmap(*a), which, 0, 0))


def _inproj_kernel(x_ref, g_ref, sc_ref, sh_ref, w_ref, o_ref, k_ref, v_ref, ik_ref, h_scr):
    j = pl.program_id(2)

    @pl.when(j == 0)
    def _():
        h_scr[...] = _norm_mod(x_ref[...], g_ref[...], sc_ref[...], sh_ref[...])

    nb, tt, tn = o_ref.shape
    res = jnp.dot(h_scr[...], w_ref[...], preferred_element_type=F32)
    o_ref[...] = res.reshape(nb, tt, tn)

    assert SEC_KK // tn == SEC_IK // tn

    @pl.when(j == SEC_KK // tn)
    def _():
        for ref, sec in ((k_ref, SEC_KK), (v_ref, SEC_VV), (ik_ref, SEC_IK)):
            c0 = sec % tn
            ref[...] = res[:, c0:c0 + ref.shape[-1]].reshape(ref.shape)


def _inproj(x, g, mod, w):
    B, T, D = x.shape
    nb, tt = _token_blocks(B, T)
    width = w.shape[1]
    tn = width // 4
    assert width % tn == 0 and tn % LANES == 0
    grid = (B // nb, T // tt, width // tn)

    def row_out(width_):
        return pl.BlockSpec((nb, tt, width_), lambda b, t, j: (b, t, 0))

    return pl.pallas_call(
        _inproj_kernel,
        grid=grid,
        in_specs=[
            pl.BlockSpec((nb, tt, D), lambda b, t, j: (b, t, 0)),
            pl.BlockSpec((1, D), lambda b, t, j: (0, 0)),
            _mod_spec(nb, D, 1, lambda b, t, j: b),
            _mod_spec(nb, D, 0, lambda b, t, j: b),
            pl.BlockSpec((D, tn), lambda b, t, j: (0, j)),
        ],
        out_specs=[pl.BlockSpec((nb, tt, tn), lambda b, t, j: (b, t, j)),
                   row_out(HEAD_DIM), row_out(HEAD_DIM), row_out(IDX_DIM)],
        out_shape=[jax.ShapeDtypeStruct((B, T, width), F32),
                   jax.ShapeDtypeStruct((B, T, HEAD_DIM), F32),
                   jax.ShapeDtypeStruct((B, T, HEAD_DIM), F32),
                   jax.ShapeDtypeStruct((B, T, IDX_DIM), F32)],
        scratch_shapes=[pltpu.VMEM((nb * tt, D), BF16)],
        compiler_params=_cparams("arbitrary", "arbitrary", "arbitrary"),
        name="inproj",
    )(x, g, mod, mod, w)


def _rearrange_w_in(w_in):
    d = w_in.shape[0]
    sizes = (N_HEADS * HEAD_DIM, HEAD_DIM, HEAD_DIM, IDX_HEADS * IDX_DIM, IDX_DIM, IDX_HEADS,
             2 * D_CONF, 3 * D_SC, N_BRANCH * d)
    offs = np.cumsum((0,) + sizes)
    q, k, v, iq, ik, iw, glu, sc, gates = [w_in[:, offs[i]:offs[i + 1]] for i in range(9)]
    iw_pad = jnp.zeros((d, LANES - IDX_HEADS), w_in.dtype)
    out = jnp.concatenate([q, iq, k, k, v, v, ik, ik, iw, iw_pad, glu, sc, gates], axis=1)
    assert out.shape[1] == P_WIDTH
    return out.astype(BF16)


def _stack_heads(x):
    rows = x.shape[0]
    lane = lax.broadcasted_iota(jnp.int32, (rows, LANES), 1)
    lo = lane < HEAD_DIM
    parts = []
    for h in range(N_HEADS):
        slab = x[:, (h // 2) * LANES:(h // 2 + 1) * LANES]
        parts.append(jnp.where(lo if h % 2 == 0 else jnp.logical_not(lo), slab, 0.0))
    return jnp.concatenate(parts, axis=0).astype(BF16)


def _count(mask):
    return jnp.sum(jnp.where(mask, 1.0, 0.0), axis=1, keepdims=True)


def _attn_kernel(q_ref, iq_ref, iw_ref, kk_ref, vv_ref, ik_ref, o_ref, key_scr, sel_scr, p_scr,
                 *, topk, n_keys, causal):
    R = CHUNK
    L = kk_ref.shape[0]
    nt = (((1,), (1,)), ((), ()))
    if causal:
        limit = (pl.program_id(1) + 1) * CHUNK
    else:
        limit = n_keys
    pos = lax.broadcasted_iota(jnp.int32, (R, L), 1)
    admissible = pos < limit

    rel = lax.dot_general(_stack_heads(iq_ref[...]), ik_ref[...].astype(BF16), nt,
                          preferred_element_type=F32)
    iw = iw_ref[...]
    score = jnp.zeros((R, L), F32)
    for h in range(IDX_HEADS):
        r = jnp.maximum(rel[h * R:(h + 1) * R] * (IDX_DIM ** -0.5), 0.0)
        score = score + (iw[:, h:h + 1] * (IDX_HEADS ** -0.5)) * r

    bits = lax.bitcast_convert_type(score, jnp.int32)
    key = jnp.where(bits < 0, bits ^ jnp.int32(0x7FFFFFFF), bits)
    key_scr[...] = jnp.where(admissible, key, INT_MIN)

    def search(i, thr):
        cand = thr + lax.shift_left(jnp.int32(1), 31 - i)
        cnt = _count(key_scr[...] >= cand)
        return jnp.where(cnt >= topk, cand, thr)

    thr = lax.fori_loop(0, 32, search, jnp.full((R, 1), INT_MIN, jnp.int32))
    keys = key_scr[...]
    n_ge = _count(keys >= thr)
    has_tie = jnp.max(n_ge) > topk

    @pl.when(jnp.logical_not(has_tie))
    def _():
        sel_scr[...] = jnp.where(jnp.logical_and(key_scr[...] >= thr, admissible), 1.0, 0.0)

    @pl.when(has_tie)
    def _():
        kk_ = key_scr[...]
        eq = kk_ == thr
        need = topk - _count(kk_ > thr)
        nbits = int(L - 1).bit_length()

        def tie_search(i, d):
            cand = d + lax.shift_left(jnp.int32(1), nbits - 1 - i)
            f = _count(jnp.logical_and(eq, pos < cand))
            return jnp.where(f < need, cand, d)

        d = lax.fori_loop(0, nbits, tie_search, jnp.zeros((R, 1), jnp.int32))
        chosen = jnp.logical_or(kk_ > thr, jnp.logical_and(eq, pos <= d))
        sel_scr[...] = jnp.where(jnp.logical_and(chosen, admissible), 1.0, 0.0)

    logits = lax.dot_general(_stack_heads(q_ref[...]), kk_ref[...].astype(BF16), nt,
                             preferred_element_type=F32) * (HEAD_DIM ** -0.5)
    sel = sel_scr[...] > 0.5
    denoms = []
    for h in range(N_HEADS):
        lg = jnp.where(sel, logits[h * R:(h + 1) * R], -jnp.inf)
        m = jnp.max(lg, axis=1, keepdims=True)
        p = jnp.exp(lg - m)
        denoms.append(jnp.sum(p, axis=1, keepdims=True))
        p_scr[h * R:(h + 1) * R, :] = p.astype(BF16)
    o = jnp.dot(p_scr[...], vv_ref[...].astype(BF16), preferred_element_type=F32)
    lo = lax.broadcasted_iota(jnp.int32, (R, LANES), 1) < HEAD_DIM
    for hp in range(N_HEADS // 2):
        even = o[(2 * hp) * R:(2 * hp + 1) * R] / denoms[2 * hp]
        odd = o[(2 * hp + 1) * R:(2 * hp + 2) * R] / denoms[2 * hp + 1]
        o_ref[:, hp * LANES:(hp + 1) * LANES] = jnp.where(lo, even, odd).astype(BF16)


def _attention(P, kk, vv, ik, kv_blocks, *, topk, n_keys, causal):
    B, T, _ = P.shape
    L = kk.shape[1]
    nq = T // CHUNK
    dq = N_HEADS * HEAD_DIM
    kern = functools.partial(_attn_kernel, topk=topk, n_keys=n_keys, causal=causal)
    ck, cv, ci = kv_blocks
    return pl.pallas_call(
        kern,
        grid=(B, nq),
        in_specs=[
            pl.BlockSpec((None, CHUNK, dq), lambda b, j: (b, j, SEC_Q // dq)),
            pl.BlockSpec((None, CHUNK, dq), lambda b, j: (b, j, SEC_IQ // dq)),
            pl.BlockSpec((None, CHUNK, LANES), lambda b, j: (b, j, SEC_IW // LANES)),
            pl.BlockSpec((None, L, LANES), lambda b, j: (b, 0, ck)),
            pl.BlockSpec((None, L, LANES), lambda b, j: (b, 0, cv)),
            pl.BlockSpec((None, L, LANES), lambda b, j: (b, 0, ci)),
        ],
        out_specs=pl.BlockSpec((None, CHUNK, dq), lambda b, j: (b, j, 0)),
        out_shape=jax.ShapeDtypeStruct((B, T, dq), BF16),
        scratch_shapes=[
            pltpu.VMEM((CHUNK, L), jnp.int32),
            pltpu.VMEM((CHUNK, L), F32),
            pltpu.VMEM((N_HEADS * CHUNK, L), BF16),
        ],
        compiler_params=_cparams("arbitrary", "arbitrary"),
        name="attn",
    )(P, P, P, kk, vv, ik)


QT = 2 * CHUNK
KB = 256


SUM_ROWS = 64


def _colcount(n_rows, pred):
    acc = jnp.zeros((SUM_ROWS, LANES), F32)
    for k in range(n_rows // SUM_ROWS):
        acc = acc + jnp.where(pred(slice(k * SUM_ROWS, (k + 1) * SUM_ROWS)), 1.0, 0.0)
    return jnp.sum(acc, axis=0, keepdims=True)


def _attn_t_kernel(q_ref, iq_ref, iw_ref, kk_ref, vv_ref, ik_ref, o_ref,
                   key_scr, sel_scr, lg_scr, p_scr, vve_scr, *, topk, tile0):
    L = kk_ref.shape[0]
    nkb = L // KB
    nh = N_HEADS
    nt = (((1,), (1,)), ((), ()))
    tn = (((0,), (0,)), ((), ()))
    tile = tile0 + pl.program_id(1)
    lane = lax.broadcasted_iota(jnp.int32, (1, LANES), 1)
    limit = tile * QT + jnp.where(lane < CHUNK, CHUNK, QT)

    w_t = jnp.transpose(iw_ref[...])
    w_rows = [(w_t[h:h + 1, :] * (IDX_HEADS ** -0.5)) * (IDX_DIM ** -0.5) for h in range(nh)]
    qs_i = _stack_heads(iq_ref[...])
    for kb in range(nkb):
        rows = slice(kb * KB, (kb + 1) * KB)
        rel = lax.dot_general(ik_ref[rows, :].astype(BF16), qs_i, nt, preferred_element_type=F32)
        score = jnp.zeros((KB, LANES), F32)
        for h in range(nh):
            score = score + w_rows[h] * jnp.maximum(rel[:, h * QT:(h + 1) * QT], 0.0)
        bits = lax.bitcast_convert_type(score, jnp.int32)
        key = jnp.where(bits < 0, bits ^ jnp.int32(0x7FFFFFFF), bits)
        pos = kb * KB + lax.broadcasted_iota(jnp.int32, (KB, LANES), 0)
        key_scr[rows, :] = jnp.where(pos < limit, key, INT_MIN)
        vve_scr[rows, 0:LANES] = vv_ref[rows, :].astype(BF16)
        vve_scr[rows, LANES:2 * LANES] = jnp.ones((KB, LANES), BF16)

    def search(i, thr):
        cand = thr + lax.shift_left(jnp.int32(1), 31 - i)
        cnt = _colcount(L, lambda r: key_scr[r, :] >= cand)
        return jnp.where(cnt >= topk, cand, thr)

    thr = jnp.full((1, LANES), INT_MIN, jnp.int32)
    if L > topk:
        thr = lax.fori_loop(0, 32, search, thr)
    has_tie = jnp.max(_colcount(L, lambda r: key_scr[r, :] >= thr)) > topk

    def row_pos(r):
        return r.start + lax.broadcasted_iota(jnp.int32, (r.stop - r.start, LANES), 0)

    @pl.when(jnp.logical_not(has_tie))
    def _():
        for kb in range(nkb):
            r = slice(kb * KB, (kb + 1) * KB)
            chosen = jnp.logical_and(key_scr[r, :] >= thr, row_pos(r) < limit)
            sel_scr[r, :] = jnp.where(chosen, 1.0, 0.0)

    @pl.when(has_tie)
    def _():
        need = topk - _colcount(L, lambda r: key_scr[r, :] > thr)
        nbits = int(L - 1).bit_length()

        def tie_search(i, d):
            cand = d + lax.shift_left(jnp.int32(1), nbits - 1 - i)
            f = _colcount(L, lambda r: jnp.logical_and(key_scr[r, :] == thr, row_pos(r) < cand))
            return jnp.where(f < need, cand, d)

        d = lax.fori_loop(0, nbits, tie_search, jnp.zeros((1, LANES), jnp.int32))
        for kb in range(nkb):
            r = slice(kb * KB, (kb + 1) * KB)
            keys = key_scr[r, :]
            pos = row_pos(r)
            chosen = jnp.logical_or(keys > thr, jnp.logical_and(keys == thr, pos <= d))
            sel_scr[r, :] = jnp.where(jnp.logical_and(chosen, pos < limit), 1.0, 0.0)

    qs = _stack_heads(q_ref[...] * (HEAD_DIM ** -0.5))
    m_run = [jnp.full((1, QT), -jnp.inf, F32) for _ in range(nh)]
    for kb in range(nkb):
        rows = slice(kb * KB, (kb + 1) * KB)
        lg = lax.dot_general(kk_ref[rows, :].astype(BF16), qs, nt, preferred_element_type=F32)
        sel = sel_scr[rows, :] > 0.5
        for h in range(nh):
            lgm = jnp.where(sel, lg[:, h * QT:(h + 1) * QT], -jnp.inf)
            lg_scr[rows, h * QT:(h + 1) * QT] = lgm
            m_run[h] = jnp.maximum(m_run[h], jnp.max(lgm, axis=0, keepdims=True))
    m_all = jnp.concatenate(m_run, axis=1)

    acc = jnp.zeros((nh * QT, 2 * LANES), F32)
    for kb in range(nkb):
        rows = slice(kb * KB, (kb + 1) * KB)
        p_scr[rows, :] = jnp.exp(lg_scr[rows, :] - m_all).astype(BF16)
        acc = acc + lax.dot_general(p_scr[rows, :], vve_scr[rows, :], tn, preferred_element_type=F32)
    on = acc[:, 0:LANES] / acc[:, LANES:2 * LANES]
    lo = lax.broadcasted_iota(jnp.int32, (QT, LANES), 1) < HEAD_DIM
    for hp in range(nh // 2):
        even = on[(2 * hp) * QT:(2 * hp + 1) * QT]
        odd = on[(2 * hp + 1) * QT:(2 * hp + 2) * QT]
        o_ref[:, hp * LANES:(hp + 1) * LANES] = jnp.where(lo, even, odd).astype(BF16)


def _attention_prompt(P, *, topk):
    B, T, _ = P.shape
    dq = N_HEADS * HEAD_DIM
    assert T % KB == 0 and QT == LANES
    tiles_per_class = KB // QT
    ck, cv, ci = SEC_KK // LANES, SEC_VV // LANES, SEC_IK // LANES
    outs = []
    for c in range(T // KB):
        L = KB * (c + 1)
        t0 = c * tiles_per_class

        def qspec(width, col, t0=t0):
            return pl.BlockSpec((None, QT, width), lambda b, i: (b, t0 + i, col))

        def kspec(col, L=L):
            return pl.BlockSpec((None, L, LANES), lambda b, i: (b, 0, col))

        outs.append(pl.pallas_call(
            functools.partial(_attn_t_kernel, topk=topk, tile0=t0),
            grid=(B, tiles_per_class),
            in_specs=[qspec(dq, SEC_Q // dq), qspec(dq, SEC_IQ // dq), qspec(LANES, SEC_IW // LANES),
                      kspec(ck), kspec(cv), kspec(ci)],
            out_specs=pl.BlockSpec((None, QT, dq), lambda b, i: (b, i, 0)),
            out_shape=jax.ShapeDtypeStruct((B, tiles_per_class * QT, dq), BF16),
            scratch_shapes=[
                pltpu.VMEM((L, LANES), jnp.int32),
                pltpu.VMEM((L, LANES), F32),
                pltpu.VMEM((L, N_HEADS * QT), F32),
                pltpu.VMEM((L, N_HEADS * QT), BF16),
                pltpu.VMEM((L, 2 * LANES), BF16),
            ],
            compiler_params=_cparams("arbitrary", "arbitrary"),
            name=f"attn_prompt_L{L}",
        )(P, P, P, P, P, P))
    return jnp.concatenate(outs, axis=1)


def _conv_kernel(a_ref, g_ref, pa_ref, pg_ref, hc_ref, sb_ref, sc_ref, sx_ref, psc_ref, psx_ref,
                 hs_ref, wc_ref, bc_ref, lg_ref, lb_ref, ws_ref,
                 conf_ref, short_ref, stc_ref, sts_ref, u_scr, us_scr, sh_scr, *, rows):
    t = pl.program_id(1)
    TT = a_ref.shape[0]

    @pl.when(t == 0)
    def _():
        u_scr[0:HIST_CONF, :] = hc_ref[...]
        us_scr[0:HIST_SHORT, :] = hs_ref[...]

    @pl.when(t > 0)
    def _():
        u_scr[0:HIST_CONF, :] = pa_ref[...] * _sigmoid(pg_ref[...])
        us_scr[0:HIST_SHORT, :] = psc_ref[...] * psx_ref[...]

    u_scr[HIST_CONF:, :] = a_ref[...] * _sigmoid(g_ref[...])
    us_scr[HIST_SHORT:, :] = sc_ref[...] * sx_ref[...]

    stc_ref[...] = u_scr[TT:TT + HIST_CONF, :]
    sts_ref[...] = us_scr[TT:TT + HIST_SHORT, :]

    off_c = HIST_CONF - (CONF_WIDTH - 1)
    off_s = HIST_SHORT - (SC_WIDTH - 1)

    s = jnp.zeros((TT, D_SC), F32)
    for w in range(SC_WIDTH):
        s = s + us_scr[off_s + w:off_s + w + TT, :] * ws_ref[w:w + 1, :]
    short_ref[...] = (sb_ref[...] * s).astype(BF16)

    for ph in range(SUBLANES):
        span = TT + HIST_CONF - (SUBLANES if ph else 0)
        sh_scr[ph, 0:span, :] = u_scr[ph:ph + span, :]

    for c in range(D_CONF // LANES):
        cs = slice(c * LANES, (c + 1) * LANES)

        def body(r, carry, cs=cs):
            r0 = pl.multiple_of(r * rows, rows)
            acc = jnp.zeros((rows, LANES), F32) + bc_ref[:, cs]
            for ph in range(SUBLANES):
                steps = [a for a in range(HIST_CONF // SUBLANES + 1)
                         if off_c <= ph + a * SUBLANES <= off_c + CONF_WIDTH - 1]
                a0 = steps[0]
                span = rows + (steps[-1] - a0) * SUBLANES
                win = sh_scr[ph, pl.ds(r0 + a0 * SUBLANES, span), cs]
                for a in steps:
                    w = ph + a * SUBLANES - off_c
                    tap = win[(a - a0) * SUBLANES:(a - a0) * SUBLANES + rows]
                    acc = acc + tap * wc_ref[w:w + 1, cs]
            u_scr[pl.ds(r0, rows), cs] = acc
            return carry

        lax.fori_loop(0, TT // rows, body, 0)

    conv = u_scr[0:TT, :]
    mu = jnp.mean(conv, axis=-1, keepdims=True)
    xc = conv - mu
    var = jnp.mean(xc * xc, axis=-1, keepdims=True)
    y = (xc * lax.rsqrt(var + EPS)) * lg_ref[...] + lb_ref[...]
    conf_ref[...] = _silu(y).astype(BF16)


def _convs(P, hist_conf, hist_short, w_dw_conf, b_dw_conf, ln_g, ln_b, w_dw_short):
    B, T, _ = P.shape
    TT = min(T, 512)
    nt = T // TT
    rows = 64
    C = D_CONF

    def sec(off):
        return pl.BlockSpec((None, TT, C), lambda b, t: (b, t, off // C))

    def prev(off, h):
        return pl.BlockSpec((None, h, C), lambda b, t: (b, jnp.maximum(t * (TT // h) - 1, 0), off // C))

    def hist(h):
        return pl.BlockSpec((None, h, C), lambda b, t: (b, 0, 0))

    def vec(r):
        return pl.BlockSpec((r, C), lambda b, t: (0, 0))

    wc = jnp.zeros((32, C), F32).at[:CONF_WIDTH].set(w_dw_conf)
    ws = jnp.zeros((8, C), F32).at[:SC_WIDTH].set(w_dw_short)
    outs = pl.pallas_call(
        functools.partial(_conv_kernel, rows=rows),
        grid=(B, nt),
        in_specs=[
            sec(SEC_GLU_A), sec(SEC_GLU_G), prev(SEC_GLU_A, HIST_CONF), prev(SEC_GLU_G, HIST_CONF),
            hist(HIST_CONF),
            sec(SEC_SC_B), sec(SEC_SC_C), sec(SEC_SC_X), prev(SEC_SC_C, HIST_SHORT),
            prev(SEC_SC_X, HIST_SHORT), hist(HIST_SHORT),
            vec(32), vec(1), vec(1), vec(1), vec(8),
        ],
        out_specs=[
            pl.BlockSpec((None, TT, C), lambda b, t: (b, t, 0)),
            pl.BlockSpec((None, TT, C), lambda b, t: (b, t, 0)),
            pl.BlockSpec((None, HIST_CONF, C), lambda b, t: (b, 0, 0)),
            pl.BlockSpec((None, HIST_SHORT, C), lambda b, t: (b, 0, 0)),
        ],
        out_shape=[
            jax.ShapeDtypeStruct((B, T, C), BF16),
            jax.ShapeDtypeStruct((B, T, C), BF16),
            jax.ShapeDtypeStruct((B, HIST_CONF, C), F32),
            jax.ShapeDtypeStruct((B, HIST_SHORT, C), F32),
        ],
        scratch_shapes=[
            pltpu.VMEM((HIST_CONF + TT, C), F32),
            pltpu.VMEM((HIST_SHORT + TT, C), F32),
            pltpu.VMEM((SUBLANES, HIST_CONF + TT, C), F32),
        ],
        compiler_params=_cparams("arbitrary", "arbitrary"),
        name="convs",
    )(P, P, P, P, hist_conf, P, P, P, P, P, hist_short,
      wc, b_dw_conf.reshape(1, C), ln_g.reshape(1, C), ln_b.reshape(1, C), ws)
    return outs


def _merge_kernel(attn_ref, conf_ref, short_ref, ga_ref, gb_ref, gc_ref, x_ref, gm_ref,
                  wb_ref, wo_ref, o_ref):
    nb, tt, d = x_ref.shape
    n = nb * tt

    def branch(ref, i):
        v = ref[...].reshape(n, ref.shape[-1]).astype(BF16)
        return jnp.dot(v, wb_ref[i], preferred_element_type=F32)

    merged = (_sigmoid(ga_ref[...].reshape(n, d)) * branch(attn_ref, 0)
              + _sigmoid(gb_ref[...].reshape(n, d)) * branch(conf_ref, 1)
              + _sigmoid(gc_ref[...].reshape(n, d)) * branch(short_ref, 2))
    out = jnp.dot(merged.astype(BF16), wo_ref[...], preferred_element_type=F32)
    o_ref[...] = x_ref[...] + gm_ref[...] * out.reshape(nb, tt, d)


def _merge(attn, conf, short, P, x, mod, w_branch, w_out):
    B, T, D = x.shape
    nb, tt = _token_blocks(B, T, 512)
    C = attn.shape[-1]

    def tok(width, col):
        return pl.BlockSpec((nb, tt, width), lambda b, t: (b, t, col))

    g0 = SEC_GATES // D
    return pl.pallas_call(
        _merge_kernel,
        grid=(B // nb, T // tt),
        in_specs=[
            tok(C, 0), tok(C, 0), tok(C, 0),
            tok(D, g0), tok(D, g0 + 1), tok(D, g0 + 2),
            tok(D, 0),
            _mod_spec(nb, D, 2, lambda b, t: b),
            pl.BlockSpec((N_BRANCH, C, D), lambda b, t: (0, 0, 0)),
            pl.BlockSpec((D, D), lambda b, t: (0, 0)),
        ],
        out_specs=tok(D, 0),
        out_shape=jax.ShapeDtypeStruct((B, T, D), F32),
        compiler_params=_cparams("arbitrary", "arbitrary"),
        name="merge",
    )(attn, conf, short, P, P, P, x, mod, w_branch, w_out)


def _ffn_kernel(x_ref, g_ref, sc_ref, sh_ref, gf_ref, wg_ref, wu_ref, wo_ref, o_ref, h_scr, acc_scr):
    f = pl.program_id(2)

    @pl.when(f == 0)
    def _():
        h_scr[...] = _norm_mod(x_ref[...], g_ref[...], sc_ref[...], sh_ref[...])
        acc_scr[...] = jnp.zeros_like(acc_scr)

    h = h_scr[...]
    gate = jnp.dot(h, wg_ref[...], preferred_element_type=F32)
    up = jnp.dot(h, wu_ref[...], preferred_element_type=F32)
    act = (_silu(gate) * up).astype(BF16)
    acc_scr[...] += jnp.dot(act, wo_ref[...], preferred_element_type=F32)

    @pl.when(f == pl.num_programs(2) - 1)
    def _():
        nb, tt, d = x_ref.shape
        o_ref[...] = x_ref[...] + gf_ref[...] * acc_scr[...].reshape(nb, tt, d)


def _ffn(x, g, mod, w_in, w_out):
    B, T, D = x.shape
    nb, tt = _token_blocks(B, T)
    dff = w_out.shape[0]
    tf = dff // 2
    assert tf % LANES == 0
    nf = dff // tf
    return pl.pallas_call(
        _ffn_kernel,
        grid=(B // nb, T // tt, nf),
        in_specs=[
            pl.BlockSpec((nb, tt, D), lambda b, t, f: (b, t, 0)),
            pl.BlockSpec((1, D), lambda b, t, f: (0, 0)),
            _mod_spec(nb, D, 4, lambda b, t, f: b),
            _mod_spec(nb, D, 3, lambda b, t, f: b),
            _mod_spec(nb, D, 5, lambda b, t, f: b),
            pl.BlockSpec((D, tf), lambda b, t, f: (0, f)),
            pl.BlockSpec((D, tf), lambda b, t, f: (0, nf + f)),
            pl.BlockSpec((tf, D), lambda b, t, f: (f, 0)),
        ],
        out_specs=pl.BlockSpec((nb, tt, D), lambda b, t, f: (b, t, 0)),
        out_shape=jax.ShapeDtypeStruct((B, T, D), F32),
        scratch_shapes=[pltpu.VMEM((nb * tt, D), BF16), pltpu.VMEM((nb * tt, D), F32)],
        compiler_params=_cparams("arbitrary", "arbitrary", "arbitrary"),
        name="ffn",
    )(x, g, mod, mod, mod, w_in, w_in, w_out)


MOE_CHUNK = 128


def _moe_kernel(x_ref, g_ref, sc_ref, sh_ref, gf_ref, wr_ref, br_ref, wg_ref, wu_ref, wo_ref,
                o_ref, h_scr, comb_scr, rank_scr, rank_t_scr, cnt_scr, acc_scr):
    e = pl.program_id(2)
    tm = h_scr.shape[0]
    lane = lax.broadcasted_iota(jnp.int32, (tm, LANES), 1)

    @pl.when(e == 0)
    def _():
        h = _norm_mod(x_ref[...], g_ref[...], sc_ref[...], sh_ref[...])
        h_scr[...] = h
        acc_scr[...] = jnp.zeros_like(acc_scr)
        logits = jnp.dot(h, wr_ref[...], preferred_element_type=F32) + br_ref[...]
        logits = jnp.where(lane < N_EXPERTS, logits, -jnp.inf)
        m1 = jnp.max(logits, axis=1, keepdims=True)
        i1 = jnp.min(jnp.where(logits == m1, lane, LANES), axis=1, keepdims=True)
        rest = jnp.where(lane == i1, -jnp.inf, logits)
        m2 = jnp.max(rest, axis=1, keepdims=True)
        i2 = jnp.min(jnp.where(rest == m2, lane, LANES), axis=1, keepdims=True)
        e2 = jnp.exp(m2 - m1)
        w1 = 1.0 / (1.0 + e2)
        w2 = e2 / (1.0 + e2)
        comb = jnp.where(lane == i1, w1, 0.0) + jnp.where(lane == i2, w2, 0.0)
        comb_scr[...] = comb
        routed = comb > 0.0
        before = (lax.broadcasted_iota(jnp.int32, (tm, tm), 1)
                  < lax.broadcasted_iota(jnp.int32, (tm, tm), 0))
        rank = jnp.dot(jnp.where(before, 1.0, 0.0).astype(BF16),
                       jnp.where(routed, 1.0, 0.0).astype(BF16), preferred_element_type=F32)
        rank = jnp.where(routed, rank, -1.0)
        rank_scr[...] = rank
        rank_t = jnp.transpose(rank)
        for ex in range(N_EXPERTS):
            rank_t_scr[ex] = jnp.broadcast_to(rank_t[ex:ex + 1, :], (SUBLANES, tm))
        cnt_scr[...] = jnp.broadcast_to(
            jnp.sum(jnp.where(routed, 1.0, 0.0), axis=0, keepdims=True), (SUBLANES, LANES))

    lane8 = lax.broadcasted_iota(jnp.int32, (SUBLANES, LANES), 1)
    n_rows = jnp.max(jnp.where(lane8 == e, cnt_scr[...], 0.0)).astype(jnp.int32)
    n_chunks = (n_rows + (MOE_CHUNK - 1)) // MOE_CHUNK
    mine = lane == e
    rank_col = jnp.sum(jnp.where(mine, rank_scr[...], 0.0), axis=1, keepdims=True)
    comb_col = jnp.sum(jnp.where(mine, comb_scr[...], 0.0), axis=1, keepdims=True)
    rank_row = rank_t_scr[e][0:1, :]
    slot_sub = lax.broadcasted_iota(jnp.int32, (MOE_CHUNK, tm), 0).astype(F32)
    slot_lane = lax.broadcasted_iota(jnp.int32, (tm, 2 * MOE_CHUNK), 1)
    slot_lane = jnp.where(slot_lane >= MOE_CHUNK, slot_lane - MOE_CHUNK, slot_lane).astype(F32)

    def chunk(c, carry):
        base = (c * MOE_CHUNK).astype(F32)
        gather = jnp.where(rank_row == slot_sub + base, 1.0, 0.0).astype(BF16)
        hs = jnp.dot(gather, h_scr[...], preferred_element_type=F32).astype(BF16)
        gate = jnp.dot(hs, wg_ref[...], preferred_element_type=F32)
        up = jnp.dot(hs, wu_ref[...], preferred_element_type=F32)
        act = (_silu(gate) * up).astype(BF16)
        y = jnp.dot(act, wo_ref[...], preferred_element_type=F32)
        y_hi = y.astype(BF16)
        y_lo = (y - y_hi.astype(F32)).astype(BF16)
        scatter = jnp.where(rank_col == slot_lane + base, 1.0, 0.0).astype(BF16)
        back = jnp.dot(scatter, jnp.concatenate([y_hi, y_lo], axis=0), preferred_element_type=F32)
        acc_scr[...] += comb_col * back
        return carry

    lax.fori_loop(0, n_chunks, chunk, 0)

    @pl.when(e == pl.num_programs(2) - 1)
    def _():
        nb, tt, d = x_ref.shape
        o_ref[...] = x_ref[...] + gf_ref[...] * acc_scr[...].reshape(nb, tt, d)


def _moe(x, g, mod, w_router, b_router, w_exp_in, w_exp_out):
    B, T, D = x.shape
    nb, tt = _token_blocks(B, T)
    ne, dfe, _ = w_exp_out.shape
    assert ne == N_EXPERTS and dfe % LANES == 0
    wr = jnp.zeros((D, LANES), BF16).at[:, :ne].set(w_router.astype(BF16))
    br = jnp.zeros((1, LANES), F32).at[0, :ne].set(b_router)
    return pl.pallas_call(
        _moe_kernel,
        grid=(B // nb, T // tt, ne),
        in_specs=[
            pl.BlockSpec((nb, tt, D), lambda b, t, e: (b, t, 0)),
            pl.BlockSpec((1, D), lambda b, t, e: (0, 0)),
            _mod_spec(nb, D, 4, lambda b, t, e: b),
            _mod_spec(nb, D, 3, lambda b, t, e: b),
            _mod_spec(nb, D, 5, lambda b, t, e: b),
            pl.BlockSpec((D, LANES), lambda b, t, e: (0, 0)),
            pl.BlockSpec((1, LANES), lambda b, t, e: (0, 0)),
            pl.BlockSpec((None, D, dfe), lambda b, t, e: (e, 0, 0)),
            pl.BlockSpec((None, D, dfe), lambda b, t, e: (e, 0, 1)),
            pl.BlockSpec((None, dfe, D), lambda b, t, e: (e, 0, 0)),
        ],
        out_specs=pl.BlockSpec((nb, tt, D), lambda b, t, e: (b, t, 0)),
        out_shape=jax.ShapeDtypeStruct((B, T, D), F32),
        scratch_shapes=[pltpu.VMEM((nb * tt, D), BF16), pltpu.VMEM((nb * tt, LANES), F32),
                        pltpu.VMEM((nb * tt, LANES), F32),
                        pltpu.VMEM((N_EXPERTS, SUBLANES, nb * tt), F32),
                        pltpu.VMEM((SUBLANES, LANES), F32),
                        pltpu.VMEM((nb * tt, D), F32)],
        compiler_params=_cparams("arbitrary", "arbitrary", "arbitrary"),
        name="moe",
    )(x, g, mod, mod, mod, wr, br, w_exp_in, w_exp_in, w_exp_out)


def _final_kernel(x_ref, g_ref, o_ref):
    x = x_ref[...]
    ms = jnp.mean(x * x, axis=-1, keepdims=True)
    o_ref[...] = (x * lax.rsqrt(ms + EPS)) * g_ref[...]


def _final_norm(x, g):
    B, T, D = x.shape
    nb, tt = _token_blocks(B, T)
    return pl.pallas_call(
        _final_kernel,
        grid=(B // nb, T // tt),
        in_specs=[pl.BlockSpec((nb, tt, D), lambda b, t: (b, t, 0)),
                  pl.BlockSpec((1, D), lambda b, t: (0, 0))],
        out_specs=pl.BlockSpec((nb, tt, D), lambda b, t: (b, t, 0)),
        out_shape=jax.ShapeDtypeStruct((B, T, D), F32),
        compiler_params=_cparams("arbitrary", "arbitrary"),
        name="final_norm",
    )(x, g.reshape(1, D))


def _dup(a):
    return jnp.concatenate([a, a], axis=-1)


def _pad_rows(a, rows):
    return jnp.pad(a, ((0, 0), (0, rows - a.shape[1]), (0, 0)))


def _trunk(x, mods, past, p):
    B, T, D = x.shape
    depth = p["w_in"].shape[0]
    states = [[], [], [], [], []]
    for l in range(depth):
        mod = mods[l]
        P, k_new, v_new, ik_new = _inproj(x, p["g_norm_mix"][l].reshape(1, D), mod, p["w_in"][l])
        if past is None:
            attn = _attention_prompt(P, topk=min(TOPK_MAX, T // 4))
            hist_conf = jnp.zeros((B, HIST_CONF, D_CONF), F32)
            hist_short = jnp.zeros((B, HIST_SHORT, D_SC), F32)
        else:
            cache_k, cache_v, cache_ik, conf_hist, short_hist = [a[l] for a in past]
            n_keys = cache_k.shape[1] + T
            lpad = -(-n_keys // LANES) * LANES
            kk = _pad_rows(_dup(jnp.concatenate([cache_k, k_new], axis=1)), lpad)
            vv = _pad_rows(_dup(jnp.concatenate([cache_v, v_new], axis=1)), lpad)
            ik = _pad_rows(_dup(jnp.concatenate([cache_ik, ik_new], axis=1)), lpad)
            attn = _attention(P, kk, vv, ik, (0, 0, 0),
                              topk=min(TOPK_MAX, n_keys // 4), n_keys=n_keys, causal=False)
            hist_conf = jnp.pad(conf_hist, ((0, 0), (HIST_CONF - (CONF_WIDTH - 1), 0), (0, 0)))
            hist_short = jnp.pad(short_hist, ((0, 0), (HIST_SHORT - (SC_WIDTH - 1), 0), (0, 0)))
        conf, short, st_conf, st_short = _convs(
            P, hist_conf, hist_short, p["w_dw_conf"][l], p["b_dw_conf"][l], p["ln_conf_g"][l],
            p["ln_conf_b"][l], p["w_dw_short"][l])
        x = _merge(attn, conf, short, P, x, mod, p["w_branch"][l], p["w_out"][l])
        if l % 2 == 0:
            x = _ffn(x, p["g_norm_ffn"][l].reshape(1, D), mod, p["w_ffn_in"][l // 2], p["w_ffn_out"][l // 2])
        else:
            x = _moe(x, p["g_norm_ffn"][l].reshape(1, D), mod, p["w_router"][l // 2], p["b_router"][l // 2],
                     p["w_exp_in"][l // 2], p["w_exp_out"][l // 2])
        for i, s in enumerate((k_new, v_new, ik_new, st_conf[:, HIST_CONF - (CONF_WIDTH - 1):],
                               st_short[:, HIST_SHORT - (SC_WIDTH - 1):])):
            states[i].append(s)
    y = _final_norm(x, p["g_final"])
    return y, tuple(jnp.stack(s) for s in states)


def kernel(x_prompt, x_sample, cache_k, cache_v, cache_idx_k, state_conv_conformer, state_conv_short,
           c_prompt, c_sample, w_ada, b_ada, g_norm_mix, g_norm_ffn, w_in, w_dw_conf, b_dw_conf,
           ln_conf_g, ln_conf_b, w_dw_short, w_branch, w_out, w_ffn_in, w_ffn_out, w_router, b_router,
           w_exp_in, w_exp_out, g_final):
    depth, D, _ = w_in.shape
    bp = x_prompt.shape[0]
    mods = _ada(jnp.concatenate([c_prompt, c_sample], axis=0), w_ada, b_ada)
    mods = mods.reshape(depth, -1, 6, 1, D)
    params = {
        "g_norm_mix": g_norm_mix, "g_norm_ffn": g_norm_ffn,
        "w_in": jnp.stack([_rearrange_w_in(w_in[l]) for l in range(depth)]),
        "w_dw_conf": w_dw_conf, "b_dw_conf": b_dw_conf, "ln_conf_g": ln_conf_g, "ln_conf_b": ln_conf_b,
        "w_dw_short": w_dw_short, "w_branch": w_branch.astype(BF16), "w_out": w_out.astype(BF16),
        "w_ffn_in": w_ffn_in.astype(BF16), "w_ffn_out": w_ffn_out.astype(BF16),
        "w_router": w_router, "b_router": b_router,
        "w_exp_in": w_exp_in.astype(BF16), "w_exp_out": w_exp_out.astype(BF16), "g_final": g_final,
    }
    y_p, (pk, pv, pik, pconf, pshort) = _trunk(x_prompt, mods[:, :bp], None, params)
    y_s, (sk, sv, sik, sconf, sshort) = _trunk(
        x_sample, mods[:, bp:],
        (cache_k, cache_v, cache_idx_k, state_conv_conformer, state_conv_short), params)
    return (y_p, y_s, pk, pv, pik, pconf, pshort, sk, sv, sik, sconf, sshort)
```
